```python
import math
import jax
import jax.numpy as jnp
from jax import lax
import numpy as np

D_MODEL = 4096
BATCH = 8
SEQ = 2048
DEPTH = 4
DEC_BATCH = 16
DEC_SEQ = 32
PAST_LEN = 2048

CHUNK = 64
SUB_CHUNK = 16
Q_BLOCK = 128
N_MIXERS = 3
N_GDN_LAYERS = (DEPTH + 2) // 3
N_FOX_LAYERS = (DEPTH + 1) // 3
N_GLA_LAYERS = DEPTH // 3

GDN_HEAD_DIM = 128
GDN_V_HEADS = D_MODEL // GDN_HEAD_DIM
GDN_QK_HEADS = GDN_V_HEADS // 2
GDN_QK_DIM = GDN_QK_HEADS * GDN_HEAD_DIM
GDN_V_DIM = GDN_V_HEADS * GDN_HEAD_DIM
GDN_CONV_DIM = 2 * GDN_QK_DIM + GDN_V_DIM
GDN_IN_DIM = GDN_CONV_DIM + GDN_V_DIM + 2 * GDN_V_HEADS
CONV_WIDTH = 4

FOX_HEAD_DIM = 128
FOX_HEADS = D_MODEL // FOX_HEAD_DIM
FOX_DIM = FOX_HEADS * FOX_HEAD_DIM
FOX_IN_DIM = 3 * FOX_DIM + FOX_HEADS
FOX_GATE_BIAS = 3.0

GLA_HEADS = 4
GLA_K_DIM = D_MODEL // 2
GLA_V_DIM = D_MODEL
GLA_HEAD_K = GLA_K_DIM // GLA_HEADS
GLA_HEAD_V = GLA_V_DIM // GLA_HEADS
GLA_GATE_RANK = 16
GLA_TAU = 16.0
GLA_IN_DIM = 2 * GLA_K_DIM + 2 * GLA_V_DIM + GLA_GATE_RANK

MEM_LEN = 256
MEM_HEADS = 4
MEM_HEAD_DIM = 128
MEM_DIM = MEM_HEADS * MEM_HEAD_DIM

D_FF = 4 * D_MODEL
NORM_EPS = 1e-6

kernel_name = 'hybrid_gdn_fox_gla_stream_step'


def _rmsnorm(x, g):
    xf = x.astype(jnp.float32)
    y = xf * lax.rsqrt(jnp.mean(xf * xf, axis=-1, keepdims=True) + NORM_EPS)
    return (y * g.astype(jnp.float32)).astype(x.dtype)


def _l2norm(x):
    xf = x.astype(jnp.float32)
    return xf * lax.rsqrt(jnp.sum(xf * xf, axis=-1, keepdims=True) + NORM_EPS)


def _chunk_len(t):
    return t if t <= CHUNK else CHUNK


def _to_chunks(a, c):
    b, t = a.shape[:2]
    a = a.reshape((b, t // c, c) + a.shape[2:])
    return a.transpose((1, 0, 3, 2) + tuple(range(4, a.ndim)))


def _from_chunks(a):
    n, b, h, c = a.shape[:4]
    a = a.transpose((1, 0, 3, 2) + tuple(range(4, a.ndim)))
    return a.reshape((b, n * c, h) + a.shape[4:])


def _causal_conv(u, buf, w):
    t = u.shape[1]
    full = jnp.concatenate([buf.astype(u.dtype), u], axis=1)
    out = full[:, CONV_WIDTH - 1:] * w[CONV_WIDTH - 1]
    for i in range(CONV_WIDTH - 1):
        out = out + full[:, i:i + t] * w[i]
    return out, full[:, full.shape[1] - (CONV_WIDTH - 1):]


def _gated_delta_rule(q, k, v, g, beta, s0):
    c = _chunk_len(q.shape[1])
    dk = q.shape[-1]
    qc, kc, vc = _to_chunks(q, c), _to_chunks(k, c), _to_chunks(v, c)
    gc = jnp.cumsum(_to_chunks(g, c), axis=-1)
    bc = _to_chunks(beta, c)
    incl = jnp.tril(jnp.ones((c, c), dtype=bool))
    strict = jnp.tril(jnp.ones((c, c), dtype=bool), -1)
    decay = jnp.exp(jnp.where(incl, gc[..., :, None] - gc[..., None, :], -jnp.inf))
    a = jnp.where(strict, bc[..., :, None] * jnp.einsum('nbhtd,nbhsd->nbhts', kc, kc) * decay, 0.0)
    lhs = a + jnp.eye(c, dtype=a.dtype)
    rhs = jnp.concatenate([(bc * jnp.exp(gc))[..., None] * kc, bc[..., None] * vc], axis=-1)
    sol = lax.linalg.triangular_solve(lhs, rhs, left_side=True, lower=True, unit_diagonal=True)
    w_dec, u_val = sol[..., :dk], sol[..., dk:]
    p = jnp.einsum('nbhtd,nbhsd->nbhts', qc, kc) * decay
    q_dec = qc * jnp.exp(gc)[..., None]
    k_tail = kc * jnp.exp(gc[..., -1:] - gc)[..., None]
    g_last = jnp.exp(gc[..., -1])

    def step(s, inp):
        w_i, u_i, p_i, qd_i, kt_i, gl_i = inp
        u_new = u_i - jnp.einsum('bhtk,bhkv->bhtv', w_i, s)
        o = jnp.einsum('bhtk,bhkv->bhtv', qd_i, s) + jnp.einsum('bhts,bhsv->bhtv', p_i, u_new)
        s = s * gl_i[..., None, None] + jnp.einsum('bhsk,bhsv->bhkv', kt_i, u_new)
        return s, o

    s, o = lax.scan(step, s0, (w_dec, u_val, p, q_dec, k_tail, g_last))
    return _from_chunks(o), s


def _gdn_mixer(h, conv_buf, s0, w_in, conv_w, a_log, dt_bias, norm_g, w_out):
    b, t, _ = h.shape
    proj = h @ w_in
    qkv, z, b_logit, a_logit = jnp.split(
        proj, [GDN_CONV_DIM, GDN_CONV_DIM + GDN_V_DIM, GDN_CONV_DIM + GDN_V_DIM + GDN_V_HEADS], axis=-1)
    qkv, new_buf = _causal_conv(qkv, conv_buf, conv_w)
    qkv = jax.nn.silu(qkv)
    q, k, v = jnp.split(qkv, [GDN_QK_DIM, 2 * GDN_QK_DIM], axis=-1)
    rep = GDN_V_HEADS // GDN_QK_HEADS
    q = jnp.repeat(_l2norm(q.reshape(b, t, GDN_QK_HEADS, GDN_HEAD_DIM)) * GDN_HEAD_DIM ** -0.5, rep, axis=2)
    k = jnp.repeat(_l2norm(k.reshape(b, t, GDN_QK_HEADS, GDN_HEAD_DIM)), rep, axis=2)
    v = v.reshape(b, t, GDN_V_HEADS, GDN_HEAD_DIM).astype(jnp.float32)
    beta = jax.nn.sigmoid(b_logit.astype(jnp.float32))
    g = -jnp.exp(a_log.astype(jnp.float32)) * jax.nn.softplus(
        a_logit.astype(jnp.float32) + dt_bias.astype(jnp.float32))
    o, s = _gated_delta_rule(q, k, v, g, beta, s0.astype(jnp.float32))
    o = _rmsnorm(o, norm_g) * jax.nn.silu(z.reshape(b, t, GDN_V_HEADS, GDN_HEAD_DIM).astype(jnp.float32))
    y = o.astype(h.dtype).reshape(b, t, GDN_V_DIM) @ w_out
    return y, new_buf, s.astype(h.dtype)


def _fox_attend(q, cq, qpos, k, v, ck, kpos):
    s = jnp.einsum('bqhd,bkhd->bhqk', q, k).astype(jnp.float32) * FOX_HEAD_DIM ** -0.5
    s = s + jnp.swapaxes(cq, 1, 2)[..., :, None] - jnp.swapaxes(ck, 1, 2)[..., None, :]
    s = jnp.where(kpos[None, :] <= qpos[:, None], s, -jnp.inf)
    p = jax.nn.softmax(s, axis=-1)
    return jnp.einsum('bhqk,bkhd->bqhd', p.astype(v.dtype), v)


def _fox_mixer(h, k_past, v_past, lf_past, w_in, b_f, w_out):
    b, t, _ = h.shape
    proj = h @ w_in
    q, k, v, f_logit = jnp.split(proj, [FOX_DIM, 2 * FOX_DIM, 3 * FOX_DIM], axis=-1)
    q = q.reshape(b, t, FOX_HEADS, FOX_HEAD_DIM)
    k = k.reshape(b, t, FOX_HEADS, FOX_HEAD_DIM)
    v = v.reshape(b, t, FOX_HEADS, FOX_HEAD_DIM)
    lf = jax.nn.log_sigmoid(f_logit.astype(jnp.float32) + b_f.astype(jnp.float32))
    if k_past is None:
        past = 0
        k_all, v_all, lf_all = k, v, lf
    else:
        past = k_past.shape[1]
        k_all = jnp.concatenate([k_past.astype(k.dtype), k], axis=1)
        v_all = jnp.concatenate([v_past.astype(v.dtype), v], axis=1)
        lf_all = jnp.concatenate([lf_past.astype(jnp.float32), lf], axis=1)
    c_all = jnp.cumsum(lf_all, axis=1)
    pos = jnp.arange(past + t)
    blk = min(Q_BLOCK, t)
    outs = []
    for i in range(t // blk):
        lo, hi = past + i * blk, past + (i + 1) * blk
        outs.append(_fox_attend(q[:, lo - past:hi - past], c_all[:, lo:hi], pos[lo:hi],
                                k_all[:, :hi], v_all[:, :hi], c_all[:, :hi], pos[:hi]))
    o = jnp.concatenate(outs, axis=1)
    y = o.reshape(b, t, FOX_DIM) @ w_out
    return y, k, v, lf.astype(h.dtype)


def _gla_recurrence(q, k, v, glog, s0):
    c = _chunk_len(q.shape[1])
    sub = math.gcd(c, SUB_CHUNK)
    ns = c // sub
    qc, kc, vc = _to_chunks(q, c), _to_chunks(k, c), _to_chunks(v, c)
    bcum = jnp.cumsum(_to_chunks(glog, c), axis=3)
    earlier = (jnp.arange(c) // sub)[None, :] < jnp.arange(ns)[:, None]
    tri = jnp.tril(jnp.ones((sub, sub), dtype=bool))
    eye_ns = jnp.eye(ns, dtype=jnp.float32)

    def step(s, inp):
        q_i, k_i, v_i, b_i = inp
        bsz, nh, _, dk = q_i.shape
        o = jnp.einsum('bhtk,bhkv->bhtv', q_i * jnp.exp(b_i), s)
        qs = q_i.reshape(bsz, nh, ns, sub, dk)
        ks = k_i.reshape(bsz, nh, ns, sub, dk)
        bs = b_i.reshape(bsz, nh, ns, sub, dk)
        ref = bs[:, :, :, 0]
        q_off = qs * jnp.exp(bs - ref[:, :, :, None])
        k_off = k_i[:, :, None] * jnp.exp(
            jnp.where(earlier[:, :, None], ref[:, :, :, None] - b_i[:, :, None], -jnp.inf))
        a_off = jnp.einsum('bhitk,bhijk->bhitj', q_off, k_off)
        d = jnp.exp(jnp.where(tri[:, :, None], bs[:, :, :, :, None] - bs[:, :, :, None], -jnp.inf))
        a_diag = jnp.einsum('bhitk,bhijk,bhitjk->bhitj', qs, ks, d)
        a_diag = jnp.einsum('bhitj,im->bhitmj', a_diag, eye_ns).reshape(bsz, nh, ns, sub, c)
        a = (a_off + a_diag).reshape(bsz, nh, c, c)
        o = o + jnp.einsum('bhts,bhsv->bhtv', a, v_i)
        b_last = b_i[:, :, -1]
        s = s * jnp.exp(b_last)[..., None] + jnp.einsum(
            'bhsk,bhsv->bhkv', k_i * jnp.exp(b_last[:, :, None] - b_i), v_i)
        return s, o

    s, o = lax.scan(step, s0, (qc, kc, vc, bcum))
    return _from_chunks(o), s


def _gla_mixer(h, s0, w_in, w_gate, b_gate, norm_g, w_out):
    b, t, _ = h.shape
    proj = h @ w_in
    q, k, v, r, g_lr = jnp.split(
        proj, [GLA_K_DIM, 2 * GLA_K_DIM, 2 * GLA_K_DIM + GLA_V_DIM, 2 * GLA_K_DIM + 2 * GLA_V_DIM], axis=-1)
    q = q.reshape(b, t, GLA_HEADS, GLA_HEAD_K).astype(jnp.float32) * GLA_HEAD_K ** -0.5
    k = k.reshape(b, t, GLA_HEADS, GLA_HEAD_K).astype(jnp.float32)
    v = v.reshape(b, t, GLA_HEADS, GLA_HEAD_V).astype(jnp.float32)
    glog = jax.nn.log_sigmoid((g_lr @ w_gate).astype(jnp.float32) + b_gate.astype(jnp.float32)) / GLA_TAU
    glog = glog.reshape(b, t, GLA_HEADS, GLA_HEAD_K)
    o, s = _gla_recurrence(q, k, v, glog, s0.astype(jnp.float32))
    o = _rmsnorm(o, norm_g) * jax.nn.silu(r.reshape(b, t, GLA_HEADS, GLA_HEAD_V).astype(jnp.float32))
    y = o.astype(h.dtype).reshape(b, t, GLA_V_DIM) @ w_out
    return y, s.astype(h.dtype)


def _mem_kv(mem, g, w_kv):
    b, m, _ = mem.shape
    k, v = jnp.split(_rmsnorm(mem, g) @ w_kv, 2, axis=-1)
    return k.reshape(b, m, MEM_HEADS, MEM_HEAD_DIM), v.reshape(b, m, MEM_HEADS, MEM_HEAD_DIM)


def _mem_attend(h, k, v, w_q, w_o):
    b, t, _ = h.shape
    q = (h @ w_q).reshape(b, t, MEM_HEADS, MEM_HEAD_DIM)
    s = jnp.einsum('bthd,bmhd->bhtm', q, k.astype(q.dtype)).astype(jnp.float32) * MEM_HEAD_DIM ** -0.5
    p = jax.nn.softmax(s, axis=-1)
    o = jnp.einsum('bhtm,bmhd->bthd', p.astype(h.dtype), v.astype(h.dtype))
    return o.reshape(b, t, MEM_DIM) @ w_o


def _ffn(h, w_up, w_down):
    return jnp.square(jax.nn.relu(h @ w_up)) @ w_down


def _trunk(x, mem_k, mem_v, gdn_conv, gdn_s, fox_k, fox_v, fox_lf, gla_s, p):
    b = x.shape[0]
    conv_out, gdn_out, fk_out, fv_out, flf_out, gla_out = [], [], [], [], [], []
    for i in range(DEPTH):
        j = i // N_MIXERS
        h = _rmsnorm(x, p['norm_mix'][i])
        if i % N_MIXERS == 0:
            if gdn_conv is None:
                buf = jnp.zeros((b, CONV_WIDTH - 1, GDN_CONV_DIM), x.dtype)
                s0 = jnp.zeros((b, GDN_V_HEADS, GDN_HEAD_DIM, GDN_HEAD_DIM), jnp.float32)
            else:
                buf, s0 = gdn_conv[j], gdn_s[j]
            y, new_buf, new_s = _gdn_mixer(h, buf, s0, p['gdn_w_in'][j], p['gdn_conv_w'][j], p['gdn_a_log'][j],
                                           p['gdn_dt_bias'][j], p['gdn_norm'][j], p['gdn_w_out'][j])
            conv_out.append(new_buf)
            gdn_out.append(new_s)
        elif i % N_MIXERS == 1:
            if fox_k is None:
                kp, vp, lp = None, None, None
            else:
                kp, vp, lp = fox_k[j], fox_v[j], fox_lf[j]
            y, nk, nv, nlf = _fox_mixer(h, kp, vp, lp, p['fox_w_in'][j], p['fox_b_f'][j], p['fox_w_out'][j])
            fk_out.append(nk)
            fv_out.append(nv)
            flf_out.append(nlf)
        else:
            if gla_s is None:
                s0 = jnp.zeros((b, GLA_HEADS, GLA_HEAD_K, GLA_HEAD_V), jnp.float32)
            else:
                s0 = gla_s[j]
            y, new_s = _gla_mixer(h, s0, p['gla_w_in'][j], p['gla_w_gate'][j], p['gla_b_gate'][j],
                                  p['gla_norm'][j], p['gla_w_out'][j])
            gla_out.append(new_s)
        x = x + y
        x = x + _mem_attend(_rmsnorm(x, p['norm_mem'][i]), mem_k[i], mem_v[i], p['mem_w_q'][i], p['mem_w_o'][i])
        x = x + _ffn(_rmsnorm(x, p['norm_ffn'][i]), p['ffn_w_up'][i], p['ffn_w_down'][i])
    return (_rmsnorm(x, p['norm_final']), jnp.stack(conv_out), jnp.stack(gdn_out), jnp.stack(fk_out),
            jnp.stack(fv_out), jnp.stack(flf_out), jnp.stack(gla_out))


def setup_inputs(seed: int = 0) -> dict:
    key = jax.random.key(seed)
    keys = iter(jax.random.split(key, 64))
    f32 = jnp.float32

    def nrm(shape, scale=1.0):
        return jax.random.normal(next(keys), shape, f32) * scale

    def gain(shape):
        return 1.0 + 0.05 * jax.random.normal(next(keys), shape, f32)

    na, nb, nc = N_GDN_LAYERS, N_FOX_LAYERS, N_GLA_LAYERS
    dt = jnp.exp(jax.random.uniform(next(keys), (na, GDN_V_HEADS), f32, math.log(1e-3), math.log(1e-1)))
    return {
        'x_prompt': nrm((BATCH, SEQ, D_MODEL)),
        'x_sample': nrm((DEC_BATCH, DEC_SEQ, D_MODEL)),
        'mem_prompt': nrm((BATCH, MEM_LEN, D_MODEL)),
        'state_gdn_conv': nrm((na, DEC_BATCH, CONV_WIDTH - 1, GDN_CONV_DIM)),
        'state_gdn_s': nrm((na, DEC_BATCH, GDN_V_HEADS, GDN_HEAD_DIM, GDN_HEAD_DIM), 0.05),
        'cache_fox_k': nrm((nb, DEC_BATCH, PAST_LEN, FOX_HEADS, FOX_HEAD_DIM)),
        'cache_fox_v': nrm((nb, DEC_BATCH, PAST_LEN, FOX_HEADS, FOX_HEAD_DIM)),
        'cache_fox_logf': jax.nn.log_sigmoid(FOX_GATE_BIAS + nrm((nb, DEC_BATCH, PAST_LEN, FOX_HEADS))),
        'state_gla_s': nrm((nc, DEC_BATCH, GLA_HEADS, GLA_HEAD_K, GLA_HEAD_V), 0.5),
        'cache_mem_k': nrm((DEPTH, DEC_BATCH, MEM_LEN, MEM_HEADS, MEM_HEAD_DIM)),
        'cache_mem_v': nrm((DEPTH, DEC_BATCH, MEM_LEN, MEM_HEADS, MEM_HEAD_DIM)),
        'norm_mix': gain((DEPTH, D_MODEL)),
        'norm_mem': gain((DEPTH, D_MODEL)),
        'norm_memsrc': gain((DEPTH, D_MODEL)),
        'norm_ffn': gain((DEPTH, D_MODEL)),
        'norm_final': gain((D_MODEL,)),
        'gdn_w_in': nrm((na, D_MODEL, GDN_IN_DIM), D_MODEL ** -0.5),
        'gdn_conv_w': nrm((na, CONV_WIDTH, GDN_CONV_DIM), CONV_WIDTH ** -0.5),
        'gdn_a_log': jnp.log(jax.random.uniform(next(keys), (na, GDN_V_HEADS), f32, 1.0, 16.0)),
        'gdn_dt_bias': dt + jnp.log(-jnp.expm1(-dt)),
        'gdn_norm': gain((na, GDN_HEAD_DIM)),
        'gdn_w_out': nrm((na, GDN_V_DIM, D_MODEL), GDN_V_DIM ** -0.5),
        'fox_w_in': nrm((nb, D_MODEL, FOX_IN_DIM), D_MODEL ** -0.5),
        'fox_b_f': FOX_GATE_BIAS + nrm((nb, FOX_HEADS), 0.5),
        'fox_w_out': nrm((nb, FOX_DIM, D_MODEL), FOX_DIM ** -0.5),
        'gla_w_in': nrm((nc, D_MODEL, GLA_IN_DIM), D_MODEL ** -0.5),
        'gla_w_gate': nrm((nc, GLA_GATE_RANK, GLA_K_DIM), GLA_GATE_RANK ** -0.5),
        'gla_b_gate': nrm((nc, GLA_K_DIM), 0.1),
        'gla_norm': gain((nc, GLA_HEAD_V)),
        'gla_w_out': nrm((nc, GLA_V_DIM, D_MODEL), GLA_V_DIM ** -0.5),
        'mem_w_q': nrm((DEPTH, D_MODEL, MEM_DIM), D_MODEL ** -0.5),
        'mem_w_kv': nrm((DEPTH, D_MODEL, 2 * MEM_DIM), D_MODEL ** -0.5),
        'mem_w_o': nrm((DEPTH, MEM_DIM, D_MODEL), MEM_DIM ** -0.5),
        'ffn_w_up': nrm((DEPTH, D_MODEL, D_FF), D_MODEL ** -0.5),
        'ffn_w_down': nrm((DEPTH, D_FF, D_MODEL), D_FF ** -0.5),
    }


def reference(x_prompt, x_sample, mem_prompt, state_gdn_conv, state_gdn_s, cache_fox_k, cache_fox_v,
              cache_fox_logf, state_gla_s, cache_mem_k, cache_mem_v, norm_mix, norm_mem, norm_memsrc,
              norm_ffn, norm_final, gdn_w_in, gdn_conv_w, gdn_a_log, gdn_dt_bias, gdn_norm, gdn_w_out,
              fox_w_in, fox_b_f, fox_w_out, gla_w_in, gla_w_gate, gla_b_gate, gla_norm, gla_w_out,
              mem_w_q, mem_w_kv, mem_w_o, ffn_w_up, ffn_w_down):
    p = dict(norm_mix=norm_mix, norm_mem=norm_mem, norm_ffn=norm_ffn, norm_final=norm_final,
             gdn_w_in=gdn_w_in, gdn_conv_w=gdn_conv_w, gdn_a_log=gdn_a_log, gdn_dt_bias=gdn_dt_bias,
             gdn_norm=gdn_norm, gdn_w_out=gdn_w_out, fox_w_in=fox_w_in, fox_b_f=fox_b_f, fox_w_out=fox_w_out,
             gla_w_in=gla_w_in, gla_w_gate=gla_w_gate, gla_b_gate=gla_b_gate, gla_norm=gla_norm,
             gla_w_out=gla_w_out, mem_w_q=mem_w_q, mem_w_o=mem_w_o, ffn_w_up=ffn_w_up, ffn_w_down=ffn_w_down)
    mk_list, mv_list = [], []
    for i in range(DEPTH):
        mk, mv = _mem_kv(mem_prompt, norm_memsrc[i], mem_w_kv[i])
        mk_list.append(mk)
        mv_list.append(mv)
    mem_k_p = jnp.stack(mk_list)
    mem_v_p = jnp.stack(mv_list)
    y_prompt, conv_p, gdn_s_p, fox_k_p, fox_v_p, fox_lf_p, gla_s_p = _trunk(
        x_prompt, mem_k_p, mem_v_p, None, None, None, None, None, None, p)
    y_sample, conv_s, gdn_s_s, fox_k_s, fox_v_s, fox_lf_s, gla_s_s = _trunk(
        x_sample, cache_mem_k, cache_mem_v, state_gdn_conv, state_gdn_s, cache_fox_k, cache_fox_v,
        cache_fox_logf, state_gla_s, p)
    return (y_prompt, y_sample, conv_p, gdn_s_p, fox_k_p, fox_v_p, fox_lf_p, gla_s_p, mem_k_p, mem_v_p,
            conv_s, gdn_s_s, fox_k_s, fox_v_s, fox_lf_s, gla_s_s)
```

```python
import functools
import math

import jax
import jax.numpy as jnp
from jax import lax
from jax.experimental import pallas as pl
from jax.experimental.pallas import tpu as pltpu

F32 = jnp.float32
BF16 = jnp.bfloat16
NORM_EPS = 1e-6
LANE = 128
CONV_WIDTH = 4
GDN_HEAD_DIM = 128
FOX_HEAD_DIM = 128
FOX_HEADS = 32
GLA_HEADS = 4
GLA_TAU = 16.0
GLA_SUB = 16
MEM_HEADS = 4
MEM_HEAD_DIM = 128
VMEM_LIMIT = 56 * 1024 * 1024

_NT = (((1,), (1,)), ((), ()))
_TN = (((0,), (0,)), ((), ()))


def _params(*sem):
    return pltpu.CompilerParams(dimension_semantics=sem, vmem_limit_bytes=VMEM_LIMIT)


def _pick(n, pref):
    t = pref
    while n % t:
        assert t % 2 == 0 and t > 8, (n, pref)
        t //= 2
    return t


def _dot(a, b, dims=None):
    a = a.astype(BF16)
    b = b.astype(BF16)
    if dims is None:
        return jnp.dot(a, b, preferred_element_type=F32)
    return lax.dot_general(a, b, dims, preferred_element_type=F32)


def _split3(x):
    hi = x.astype(BF16)
    r = x - hi.astype(F32)
    mid = r.astype(BF16)
    lo = (r - mid.astype(F32)).astype(BF16)
    return hi, mid, lo


def _cumsum_rows(x):
    n = x.shape[0]
    tri = (lax.broadcasted_iota(jnp.int32, (n, n), 1) <= lax.broadcasted_iota(jnp.int32, (n, n), 0))
    tri = jnp.where(tri, 1.0, 0.0).astype(BF16)
    hi, mid, lo = _split3(x)
    out = jnp.dot(tri, lo, preferred_element_type=F32)
    out = out + jnp.dot(tri, mid, preferred_element_type=F32)
    return out + jnp.dot(tri, hi, preferred_element_type=F32)


def _log_sigmoid(x):
    return jnp.minimum(x, 0.0) - jnp.log1p(jnp.exp(-jnp.abs(x)))


def _softplus(x):
    return jnp.maximum(x, 0.0) + jnp.log1p(jnp.exp(-jnp.abs(x)))


def _sigmoid(x):
    return 1.0 / (1.0 + jnp.exp(-x))


def _silu(x):
    return x * _sigmoid(x)


def _rms_rows(x, g):
    return x * lax.rsqrt(jnp.mean(x * x, axis=-1, keepdims=True) + NORM_EPS) * g


def _rms_mm_kernel(*refs, act, has_tail):
    if has_tail:
        x_ref, g_ref, w_ref, wt_ref, o_ref, ot_ref, h_ref = refs
    else:
        x_ref, g_ref, w_ref, o_ref, h_ref = refs

    @pl.when(pl.program_id(1) == 0)
    def _():
        h_ref[...] = _rms_rows(x_ref[...], g_ref[...]).astype(BF16)
        if has_tail:
            ot_ref[...] = jnp.dot(h_ref[...], wt_ref[...], preferred_element_type=F32)

    acc = jnp.dot(h_ref[...], w_ref[...], preferred_element_type=F32)
    if act == "relu2":
        acc = jnp.square(jnp.maximum(acc, 0.0))
    o_ref[...] = acc.astype(o_ref.dtype)


def rms_mm(x, g, w, w_tail=None, *, act=None, out_dtype=F32, tm=512, tn=512):
    m, k = x.shape
    n = w.shape[1]
    tm, tn = _pick(m, tm), _pick(n, tn)
    has_tail = w_tail is not None
    in_specs = [
        pl.BlockSpec((tm, k), lambda i, j: (i, 0)),
        pl.BlockSpec((1, k), lambda i, j: (0, 0)),
        pl.BlockSpec((k, tn), lambda i, j: (0, j)),
    ]
    out_specs = [pl.BlockSpec((tm, tn), lambda i, j: (i, j))]
    out_shape = [jax.ShapeDtypeStruct((m, n), out_dtype)]
    args = [x, g.reshape(1, k), w]
    if has_tail:
        in_specs.append(pl.BlockSpec((k, LANE), lambda i, j: (0, 0)))
        out_specs.append(pl.BlockSpec((tm, LANE), lambda i, j: (i, 0)))
        out_shape.append(jax.ShapeDtypeStruct((m, LANE), F32))
        args.append(w_tail)
    res = pl.pallas_call(
        functools.partial(_rms_mm_kernel, act=act, has_tail=has_tail),
        grid=(m // tm, n // tn),
        in_specs=in_specs,
        out_specs=out_specs,
        out_shape=out_shape,
        scratch_shapes=[pltpu.VMEM((tm, k), BF16)],
        compiler_params=_params("parallel", "arbitrary"),
        name="rms_mm",
    )(*args)
    return res if has_tail else res[0]


def _mm_res_kernel(a_ref, w_ref, r_ref, o_ref):
    @pl.when(pl.program_id(2) == 0)
    def _():
        o_ref[...] = r_ref[...]

    o_ref[...] += jnp.dot(a_ref[...], w_ref[...], preferred_element_type=F32)


def mm_res(a, w, res, *, tm=512, tn=512, tk=None):
    m, k = a.shape
    n = w.shape[1]
    tm, tn, tk = _pick(m, tm), _pick(n, tn), _pick(k, k if tk is None else tk)
    return pl.pallas_call(
        _mm_res_kernel,
        grid=(m // tm, n // tn, k // tk),
        in_specs=[
            pl.BlockSpec((tm, tk), lambda i, j, l: (i, l)),
            pl.BlockSpec((tk, tn), lambda i, j, l: (l, j)),
            pl.BlockSpec((tm, tn), lambda i, j, l: (i, j)),
        ],
        out_specs=pl.BlockSpec((tm, tn), lambda i, j, l: (i, j)),
        out_shape=jax.ShapeDtypeStruct((m, n), F32),
        input_output_aliases={2: 0},
        compiler_params=_params("parallel", "parallel", "arbitrary"),
        name="mm_res",
    )(a, w, res)


def _rmsnorm_kernel(x_ref, g_ref, o_ref):
    o_ref[...] = _rms_rows(x_ref[...], g_ref[...])


def rmsnorm(x, g, *, tm=512):
    m, k = x.shape
    tm = _pick(m, tm)
    return pl.pallas_call(
        _rmsnorm_kernel,
        grid=(m // tm,),
        in_specs=[pl.BlockSpec((tm, k), lambda i: (i, 0)), pl.BlockSpec((1, k), lambda i: (0, 0))],
        out_specs=pl.BlockSpec((tm, k), lambda i: (i, 0)),
        out_shape=jax.ShapeDtypeStruct((m, k), F32),
        compiler_params=_params("parallel"),
        name="rmsnorm",
    )(x, g.reshape(1, k))


def _mem_attn_kernel(x_ref, g_ref, wq_ref, k_ref, v_ref, wo_ref, o_ref):
    x = x_ref[...]
    h = _rms_rows(x, g_ref[...]).astype(BF16)
    q = jnp.dot(h, wq_ref[...], preferred_element_type=F32)
    k = k_ref[...].astype(BF16)
    v = v_ref[...].astype(BF16)
    outs = []
    for hh in range(MEM_HEADS):
        sl = slice(hh * MEM_HEAD_DIM, (hh + 1) * MEM_HEAD_DIM)
        s = _dot(q[:, sl], k[:, sl], _NT) * MEM_HEAD_DIM ** -0.5
        p = jnp.exp(s - jnp.max(s, axis=-1, keepdims=True))
        p = p / jnp.sum(p, axis=-1, keepdims=True)
        outs.append(_dot(p, v[:, sl]))
    o = jnp.concatenate(outs, axis=1).astype(BF16)
    o_ref[...] = x + jnp.dot(o, wo_ref[...], preferred_element_type=F32)


def mem_attn(x, g, wq, k, v, wo, *, row0, nb, t, tq, kmap, vmap, kblk):
    m, d = x.shape
    nq = t // tq
    r0 = row0 // tq
    dm = wq.shape[1]
    mem_len = k.shape[1]
    xmap = lambda b, i: (r0 + b * nq + i, 0)
    const = lambda b, i: (0, 0)
    return pl.pallas_call(
        _mem_attn_kernel,
        grid=(nb, nq),
        in_specs=[
            pl.BlockSpec((tq, d), xmap),
            pl.BlockSpec((1, d), const),
            pl.BlockSpec((d, dm), const),
            pl.BlockSpec((None, mem_len, kblk), kmap),
            pl.BlockSpec((None, mem_len, kblk), vmap),
            pl.BlockSpec((dm, d), const),
        ],
        out_specs=pl.BlockSpec((tq, d), xmap),
        out_shape=jax.ShapeDtypeStruct((m, d), F32),
        input_output_aliases={0: 0},
        compiler_params=_params("parallel", "parallel"),
        name="mem_attn",
    )(x, g.reshape(1, d), wq, k, v, wo)


def _fox_gate_kernel(*refs, gate, has_init):
    refs = list(refs)
    x_ref = refs.pop(0)
    bias_ref = refs.pop(0) if gate else None
    init_ref = refs.pop(0) if has_init else None
    lf_ref = refs.pop(0) if gate else None
    c_ref, carry_ref = refs

    @pl.when(pl.program_id(1) == 0)
    def _():
        carry_ref[...] = init_ref[...] if has_init else jnp.zeros_like(carry_ref)

    x = x_ref[...]
    if gate:
        x = _log_sigmoid(x + bias_ref[...])
        lf_ref[...] = x
    c = _cumsum_rows(x) + carry_ref[...]
    c_ref[...] = c
    carry_ref[...] = c[-1:, :]


def fox_gate(x, bias, init, *, row0, nb, t):
    w = x.shape[1]
    tb = min(t, 256)
    nt = t // tb
    r0 = row0 // tb
    gate = bias is not None
    has_init = init is not None
    xmap = lambda b, i: (r0 + b * nt + i, 0)
    omap = lambda b, i: (b * nt + i, 0)
    in_specs = [pl.BlockSpec((tb, w), xmap)]
    args = [x]
    if gate:
        in_specs.append(pl.BlockSpec((1, w), lambda b, i: (0, 0)))
        args.append(bias)
    if has_init:
        in_specs.append(pl.BlockSpec((None, 1, w), lambda b, i: (b, 0, 0)))
        args.append(init)
    n_out = 2 if gate else 1
    res = pl.pallas_call(
        functools.partial(_fox_gate_kernel, gate=gate, has_init=has_init),
        grid=(nb, nt),
        in_specs=in_specs,
        out_specs=[pl.BlockSpec((tb, w), omap)] * n_out,
        out_shape=[jax.ShapeDtypeStruct((nb * t, w), F32)] * n_out,
        scratch_shapes=[pltpu.VMEM((1, w), F32)],
        compiler_params=_params("parallel", "arbitrary"),
        name="fox_gate",
    )(*args)
    return res if gate else res[0]


def _fox_flash_kernel(q_ref, k_ref, v_ref, cq_ref, ck_ref, o_ref, m_ref, l_ref, acc_ref, *, bq, bk):
    qi = pl.program_id(2)
    kj = pl.program_id(3)

    @pl.when(kj == 0)
    def _():
        m_ref[...] = jnp.full_like(m_ref, -jnp.inf)
        l_ref[...] = jnp.zeros_like(l_ref)
        acc_ref[...] = jnp.zeros_like(acc_ref)

    @pl.when(kj <= qi)
    def _():
        s = _dot(q_ref[...], k_ref[...], _NT) * FOX_HEAD_DIM ** -0.5
        cq = jnp.concatenate([cq_ref[...]] * (bk // LANE), axis=1)
        s = s + (cq - ck_ref[...])
        row = lax.broadcasted_iota(jnp.int32, (bq, bk), 0) + qi * bq
        col = lax.broadcasted_iota(jnp.int32, (bq, bk), 1) + kj * bk
        s = jnp.where(col <= row, s, -jnp.inf)
        m_prev = m_ref[...]
        m_new = jnp.maximum(m_prev, jnp.max(s, axis=-1, keepdims=True))
        alpha = jnp.exp(m_prev - m_new)
        p = jnp.exp(s - m_new)
        l_ref[...] = alpha * l_ref[...] + jnp.sum(p, axis=-1, keepdims=True)
        acc_ref[...] = alpha * acc_ref[...] + _dot(p, v_ref[...])
        m_ref[...] = m_new

    @pl.when(kj == qi)
    def _():
        o_ref[...] = (acc_ref[...] / l_ref[...]).astype(o_ref.dtype)


def fox_flash(proj, cq_b, ck_row, *, nb, t, n_heads, bq=512, bk=512):
    assert bq == bk and t % bq == 0
    nq = t // bq
    d = FOX_HEAD_DIM
    return pl.pallas_call(
        functools.partial(_fox_flash_kernel, bq=bq, bk=bk),
        grid=(nb, n_heads, nq, nq),
        in_specs=[
            pl.BlockSpec((bq, d), lambda b, h, i, j: (b * nq + i, h)),
            pl.BlockSpec((bk, d), lambda b, h, i, j: (b * nq + jnp.minimum(j, i), n_heads + h)),
            pl.BlockSpec((bk, d), lambda b, h, i, j: (b * nq + jnp.minimum(j, i), 2 * n_heads + h)),
            pl.BlockSpec((None, None, bq, LANE), lambda b, h, i, j: (b, h, i, 0)),
            pl.BlockSpec((None, None, 1, bk), lambda b, h, i, j: (b, h, 0, jnp.minimum(j, i))),
        ],
        out_specs=pl.BlockSpec((bq, d), lambda b, h, i, j: (b * nq + i, h)),
        out_shape=jax.ShapeDtypeStruct((proj.shape[0], n_heads * d), BF16),
        scratch_shapes=[pltpu.VMEM((bq, 1), F32), pltpu.VMEM((bq, 1), F32), pltpu.VMEM((bq, d), F32)],
        compiler_params=_params("parallel", "parallel", "parallel", "arbitrary"),
        name="fox_flash",
    )(proj, proj, proj, cq_b, ck_row)


def _fox_sample_kernel(q_ref, kn_ref, vn_ref, kp_ref, vp_ref, cq_ref, ckp_ref, ckn_ref, prev_ref, o_ref):
    del prev_ref
    t = q_ref.shape[0]
    scale = FOX_HEAD_DIM ** -0.5
    q = q_ref[...]
    cq = cq_ref[...]
    npast = kp_ref.shape[0]
    cq_wide = jnp.concatenate([cq] * (npast // LANE), axis=1)
    s_p = _dot(q, kp_ref[...], _NT) * scale + (cq_wide - ckp_ref[...])
    s_n = _dot(q, kn_ref[...], _NT) * scale + (cq[:, :t] - ckn_ref[...])
    row = lax.broadcasted_iota(jnp.int32, (t, t), 0)
    col = lax.broadcasted_iota(jnp.int32, (t, t), 1)
    s_n = jnp.where(col <= row, s_n, -jnp.inf)
    m = jnp.maximum(jnp.max(s_p, axis=-1, keepdims=True), jnp.max(s_n, axis=-1, keepdims=True))
    p_p = jnp.exp(s_p - m)
    p_n = jnp.exp(s_n - m)
    inv = 1.0 / (jnp.sum(p_p, axis=-1, keepdims=True) + jnp.sum(p_n, axis=-1, keepdims=True))
    o = _dot(p_p * inv, vp_ref[...]) + _dot(p_n * inv, vn_ref[...])
    o_ref[...] = o.astype(o_ref.dtype)


def fox_sample(proj, k_past, v_past, cq_b, ck_past, ck_new, o_prev, *, row0, nb, t, n_heads):
    d = FOX_HEAD_DIM
    r0 = row0 // t
    npast = k_past.shape[1]
    rows = lambda b, h: (r0 + b, h)
    return pl.pallas_call(
        _fox_sample_kernel,
        grid=(nb, n_heads),
        in_specs=[
            pl.BlockSpec((t, d), rows),
            pl.BlockSpec((t, d), lambda b, h: (r0 + b, n_heads + h)),
            pl.BlockSpec((t, d), lambda b, h: (r0 + b, 2 * n_heads + h)),
            pl.BlockSpec((None, npast, d), lambda b, h: (b, 0, h)),
            pl.BlockSpec((None, npast, d), lambda b, h: (b, 0, h)),
            pl.BlockSpec((None, None, t, LANE), lambda b, h: (b, h, 0, 0)),
            pl.BlockSpec((None, None, 1, npast), lambda b, h: (b, h, 0, 0)),
            pl.BlockSpec((None, None, 1, t), lambda b, h: (b, h, 0, 0)),
            pl.BlockSpec(memory_space=pl.ANY),
        ],
        out_specs=pl.BlockSpec((t, d), rows),
        out_shape=jax.ShapeDtypeStruct(o_prev.shape, o_prev.dtype),
        input_output_aliases={8: 0},
        compiler_params=_params("parallel", "parallel"),
        name="fox_sample",
    )(proj, proj, proj, k_past.reshape(nb, npast, n_heads * d), v_past.reshape(nb, npast, n_heads * d),
      cq_b, ck_past, ck_new, o_prev)


GDN_GROUP_V = 8
GDN_INV_BASE = 16


def _transpose_lanes(x):
    eye = (lax.broadcasted_iota(jnp.int32, (LANE, LANE), 0) == lax.broadcasted_iota(jnp.int32, (LANE, LANE), 1))
    eye = jnp.where(eye, 1.0, 0.0).astype(BF16)
    hi, mid, lo = _split3(x)
    out = lax.dot_general(eye, lo, _NT, preferred_element_type=F32)
    out = out + lax.dot_general(eye, mid, _NT, preferred_element_type=F32)
    return out + lax.dot_general(eye, hi, _NT, preferred_element_type=F32)


def _unit_lower_inverse(a):
    n = a.shape[0]
    row = lax.broadcasted_iota(jnp.int32, (n, n), 0)
    col = lax.broadcasted_iota(jnp.int32, (n, n), 1)
    blk = min(GDN_INV_BASE, n)
    shift = int(math.log2(blk))
    same = (row >> shift) == (col >> shift)
    d = jnp.where(same, a, 0.0)
    t = jnp.where(row == col, 1.0, 0.0) - d
    p = d
    for _ in range(shift - 1):
        p = _dot(p, p)
        t = t + _dot(t, p)
    while blk < n:
        blk *= 2
        shift += 1
        same2 = (row >> shift) == (col >> shift)
        m = jnp.where(jnp.logical_and(same2, jnp.logical_not(same)), a, 0.0)
        t = t - _dot(_dot(t, m), t)
        same = same2
    return t


def _gdn_kernel(*refs, c, has_state):
    refs = list(refs)
    q_ref, k_ref, v_ref, z_ref, gate_ref, wq_ref, wk_ref, wv_ref, gp_ref, ng_ref = refs[:10]
    refs = refs[10:]
    if has_state:
        bq_ref, bk_ref, bv_ref, s0_ref, _prev = refs[:5]
        refs = refs[5:]
    o_ref, s_ref, xq_ref, xk_ref, xv_ref = refs
    d = GDN_HEAD_DIM

    @pl.when(pl.program_id(2) == 0)
    def _():
        for x_ref, b_ref in ((xq_ref, bq_ref if has_state else None), (xk_ref, bk_ref if has_state else None),
                             (xv_ref, bv_ref if has_state else None)):
            x_ref[0:8, :] = jnp.zeros((8, x_ref.shape[1]), F32)
            if has_state:
                x_ref[8 - (CONV_WIDTH - 1):8, :] = b_ref[...]
        s_ref[...] = s0_ref[...] if has_state else jnp.zeros_like(s_ref)

    def conv(x_ref, raw_ref, w_ref):
        x_ref[8:8 + c, :] = raw_ref[...]
        w = w_ref[...]
        out = x_ref[8:8 + c, :] * w[3:4, :]
        for i in range(CONV_WIDTH - 1):
            out = out + x_ref[5 + i:5 + i + c, :] * w[i:i + 1, :]
        x_ref[0:8, :] = x_ref[c:c + 8, :]
        return _silu(out)

    qc = conv(xq_ref, q_ref, wq_ref)
    kc = conv(xk_ref, k_ref, wk_ref)
    vc = conv(xv_ref, v_ref, wv_ref)

    gt = gate_ref[...]
    gp = gp_ref[...]
    beta_all = _sigmoid(gt)
    g_all = -jnp.exp(gp[0:1, :]) * _softplus(gt + gp[1:2, :])
    gc_all = _cumsum_rows(g_all)
    gc_t = _transpose_lanes(gc_all)

    row = lax.broadcasted_iota(jnp.int32, (c, c), 0)
    col = lax.broadcasted_iota(jnp.int32, (c, c), 1)
    incl = col <= row
    strict = col < row
    ng = ng_ref[...]
    z = z_ref[...]

    for hq in range(GDN_GROUP_V // 2):
        sl = slice(hq * d, (hq + 1) * d)
        q_h = qc[:, sl]
        k_h = kc[:, sl]
        q_h = q_h * lax.rsqrt(jnp.sum(q_h * q_h, axis=-1, keepdims=True) + NORM_EPS) * d ** -0.5
        k_h = k_h * lax.rsqrt(jnp.sum(k_h * k_h, axis=-1, keepdims=True) + NORM_EPS)
        k_b = k_h.astype(BF16)
        kk = lax.dot_general(k_b, k_b, _NT, preferred_element_type=F32)
        qk = lax.dot_general(q_h.astype(BF16), k_b, _NT, preferred_element_type=F32)
        for r in range(2):
            j = 2 * hq + r
            vsl = slice(j * d, (j + 1) * d)
            beta = beta_all[:, j:j + 1]
            gcol = gc_all[:, 8 + j:9 + j]
            grow = gc_t[8 + j:9 + j, :]
            glast = gc_all[c - 1:c, 8 + j:9 + j]
            decay = jnp.exp(jnp.where(incl, gcol - grow, -jnp.inf))
            a = jnp.where(strict, beta * kk * decay, 0.0)
            tinv = _unit_lower_inverse(a)
            eg = jnp.exp(gcol)
            rhs = jnp.concatenate([k_h * (beta * eg), vc[:, vsl] * beta], axis=1)
            sol = _dot(tinv, rhs)
            s_old = s_ref[j]
            ws = _dot(jnp.concatenate([sol[:, :d], q_h * eg], axis=0), s_old)
            u_new = sol[:, d:] - ws[:c]
            o = ws[c:] + _dot(qk * decay, u_new)
            k_tail = k_h * jnp.exp(glast - gcol)
            s_ref[j] = s_old * jnp.exp(glast) + _dot(k_tail, u_new, _TN)
            o = o * lax.rsqrt(jnp.mean(o * o, axis=-1, keepdims=True) + NORM_EPS) * ng * _silu(z[:, vsl])
            o_ref[:, vsl] = o.astype(o_ref.dtype)


def gdn_core(proj, gates, conv_w, gate_par, norm_g, conv_buf, s0, o_prev, *, row0, nb, t, c, n_v_heads):
    has_state = s0 is not None
    d = GDN_HEAD_DIM
    gv = GDN_GROUP_V
    ng = n_v_heads // gv
    wqk = (gv // 2) * d
    wv = gv * d
    nqk = (n_v_heads // 2) * d // wqk
    nc = t // c
    r0 = row0 // c
    m = proj.shape[0]
    rows = lambda b, g, i: r0 + b * nc + i
    in_specs = [
        pl.BlockSpec((c, wqk), lambda b, g, i: (rows(b, g, i), g)),
        pl.BlockSpec((c, wqk), lambda b, g, i: (rows(b, g, i), nqk + g)),
        pl.BlockSpec((c, wv), lambda b, g, i: (rows(b, g, i), ng + g)),
        pl.BlockSpec((c, wv), lambda b, g, i: (rows(b, g, i), 2 * ng + g)),
        pl.BlockSpec((None, c, LANE), lambda b, g, i: (g, rows(b, g, i), 0)),
        pl.BlockSpec((CONV_WIDTH, wqk), lambda b, g, i: (0, g)),
        pl.BlockSpec((CONV_WIDTH, wqk), lambda b, g, i: (0, nqk + g)),
        pl.BlockSpec((CONV_WIDTH, wv), lambda b, g, i: (0, ng + g)),
        pl.BlockSpec((None, 2, LANE), lambda b, g, i: (g, 0, 0)),
        pl.BlockSpec((1, d), lambda b, g, i: (0, 0)),
    ]
    args = [proj, proj, proj, proj, gates, conv_w, conv_w, conv_w, gate_par, norm_g.reshape(1, d)]
    aliases = {}
    if has_state:
        in_specs += [
            pl.BlockSpec((None, CONV_WIDTH - 1, wqk), lambda b, g, i: (b, 0, g)),
            pl.BlockSpec((None, CONV_WIDTH - 1, wqk), lambda b, g, i: (b, 0, nqk + g)),
            pl.BlockSpec((None, CONV_WIDTH - 1, wv), lambda b, g, i: (b, 0, ng + g)),
            pl.BlockSpec((None, gv, d, d), lambda b, g, i: (b, g, 0, 0)),
            pl.BlockSpec(memory_space=pl.ANY),
        ]
        args += [conv_buf, conv_buf, conv_buf, s0, o_prev]
        aliases = {14: 0}
    return pl.pallas_call(
        functools.partial(_gdn_kernel, c=c, has_state=has_state),
        grid=(nb, ng, nc),
        in_specs=in_specs,
        out_specs=[
            pl.BlockSpec((c, wv), lambda b, g, i: (rows(b, g, i), g)),
            pl.BlockSpec((None, gv, d, d), lambda b, g, i: (b, g, 0, 0)),
        ],
        out_shape=[
            jax.ShapeDtypeStruct((m, n_v_heads * d), BF16),
            jax.ShapeDtypeStruct((nb, n_v_heads, d, d), F32),
        ],
        scratch_shapes=[pltpu.VMEM((c + 8, wqk), F32), pltpu.VMEM((c + 8, wqk), F32), pltpu.VMEM((c + 8, wv), F32)],
        input_output_aliases=aliases,
        compiler_params=_params("parallel", "parallel", "arbitrary"),
        name="gdn_core",
    )(*args)


def _gla_kernel(*refs, c, has_state):
    refs = list(refs)
    q_ref, k_ref, v_ref, r_ref, glr_ref, wg_ref, bg_ref, ng_ref = refs[:8]
    refs = refs[8:]
    if has_state:
        s0_ref, _prev = refs[:2]
        refs = refs[2:]
    o_ref, s_ref, st_ref = refs
    dk = q_ref.shape[1]
    sub = min(GLA_SUB, c)
    ci = pl.program_id(2)

    @pl.when(ci == 0)
    def _():
        st_ref[...] = s0_ref[...].T if has_state else jnp.zeros_like(st_ref)

    glog = _log_sigmoid(_dot(glr_ref[...], wg_ref[...]) + bg_ref[...]) * (1.0 / GLA_TAU)
    bc = _cumsum_rows(glog)
    q = q_ref[...] * dk ** -0.5
    k = k_ref[...]
    v = v_ref[...].astype(BF16)
    st = st_ref[...]
    o = _dot(q * jnp.exp(bc), st, _NT)

    tok = lax.broadcasted_iota(jnp.int32, (c, 1), 0)
    a_rows = []
    for i in range(c // sub):
        lo = i * sub
        if i == 0:
            a_rows.append(jnp.zeros((sub, c), F32))
            continue
        ref_i = bc[lo:lo + 1, :]
        q_off = q[lo:lo + sub, :] * jnp.exp(bc[lo:lo + sub, :] - ref_i)
        k_off = k * jnp.exp(jnp.where(tok < lo, ref_i - bc, -jnp.inf))
        a_rows.append(_dot(q_off, k_off, _NT))
    a = jnp.concatenate(a_rows, axis=0)

    row = lax.broadcasted_iota(jnp.int32, (c, c), 0)
    col = lax.broadcasted_iota(jnp.int32, (c, c), 1)
    rmod = jnp.bitwise_and(row, sub - 1)
    for dlt in range(sub):
        k_r = k if dlt == 0 else pltpu.roll(k, dlt, 0)
        b_r = bc if dlt == 0 else pltpu.roll(bc, dlt, 0)
        e = jnp.exp(jnp.minimum(bc - b_r, 0.0))
        val = jnp.sum(q * k_r * e, axis=-1, keepdims=True)
        a = a + jnp.where(jnp.logical_and(col == row - dlt, rmod >= dlt), val, 0.0)

    o = o + _dot(a, v)
    b_last = bc[c - 1:c, :]
    k_tail = k * jnp.exp(b_last - bc)
    st_new = st * jnp.exp(b_last) + _dot(v, k_tail, _TN)
    st_ref[...] = st_new
    o = o * lax.rsqrt(jnp.mean(o * o, axis=-1, keepdims=True) + NORM_EPS) * ng_ref[...] * _silu(r_ref[...])
    o_ref[...] = o.astype(o_ref.dtype)

    @pl.when(ci == pl.num_programs(2) - 1)
    def _():
        s_ref[...] = st_new.T


def gla_core(proj, glr, w_gate, b_gate, norm_g, s0, o_prev, *, row0, nb, t, c, n_heads):
    has_state = s0 is not None
    m = proj.shape[0]
    dk = w_gate.shape[1] // n_heads
    dv = 2 * dk
    nc = t // c
    r0 = row0 // c
    rows = lambda b, h, i: r0 + b * nc + i
    in_specs = [
        pl.BlockSpec((c, dk), lambda b, h, i: (rows(b, h, i), h)),
        pl.BlockSpec((c, dk), lambda b, h, i: (rows(b, h, i), n_heads + h)),
        pl.BlockSpec((c, dv), lambda b, h, i: (rows(b, h, i), n_heads + h)),
        pl.BlockSpec((c, dv), lambda b, h, i: (rows(b, h, i), 2 * n_heads + h)),
        pl.BlockSpec((c, LANE), lambda b, h, i: (rows(b, h, i), 0)),
        pl.BlockSpec((LANE, dk), lambda b, h, i: (0, h)),
        pl.BlockSpec((1, dk), lambda b, h, i: (0, h)),
        pl.BlockSpec((1, dv), lambda b, h, i: (0, 0)),
    ]
    args = [proj, proj, proj, proj, glr, w_gate, b_gate.reshape(1, -1), norm_g.reshape(1, dv)]
    aliases = {}
    if has_state:
        in_specs += [
            pl.BlockSpec((None, None, dk, dv), lambda b, h, i: (b, h, 0, 0)),
            pl.BlockSpec(memory_space=pl.ANY),
        ]
        args += [s0, o_prev]
        aliases = {9: 0}
    return pl.pallas_call(
        functools.partial(_gla_kernel, c=c, has_state=has_state),
        grid=(nb, n_heads, nc),
        in_specs=in_specs,
        out_specs=[
            pl.BlockSpec((c, dv), lambda b, h, i: (rows(b, h, i), h)),
            pl.BlockSpec((None, None, dk, dv), lambda b, h, i: (b, h, 0, 0)),
        ],
        out_shape=[
            jax.ShapeDtypeStruct((m, n_heads * dv), BF16),
            jax.ShapeDtypeStruct((nb, n_heads, dk, dv), F32),
        ],
        scratch_shapes=[pltpu.VMEM((dv, dk), F32)],
        input_output_aliases=aliases,
        compiler_params=_params("parallel", "parallel", "arbitrary"),
        name="gla_core",
    )(*args)


GDN_CHUNK = 128
GLA_CHUNK = 64
FOX_BLOCK = 512
MM_TM, MM_TN, MM_TK = 1536, 1024, 1024


def _split_w_in(w_in, n_main):
    tail = w_in[:, n_main:]
    tail = jnp.pad(tail, ((0, 0), (0, LANE - tail.shape[1])))
    return w_in[:, :n_main].astype(BF16), tail.astype(BF16)


def _lanes(v, start):
    return jnp.pad(v, (start, LANE - start - v.shape[0])).reshape(1, LANE)


def _gdn_layer(x, g, w_in, conv_w, a_log, dt_bias, norm_g, w_out, conv_buf, s0, dims):
    bp, tp, bs, ts = dims
    mp, m = bp * tp, x.shape[0]
    nv = a_log.shape[0]
    conv_dim = conv_w.shape[1]
    v_dim = nv * GDN_HEAD_DIM
    w_main, w_tail = _split_w_in(w_in, conv_dim + v_dim)
    proj, tail = rms_mm(x, g, w_main, w_tail)
    ng = nv // GDN_GROUP_V
    gates = jnp.concatenate([
        tail[:, :nv].reshape(m, ng, GDN_GROUP_V), tail[:, nv:2 * nv].reshape(m, ng, GDN_GROUP_V),
        jnp.zeros((m, ng, LANE - 2 * GDN_GROUP_V), F32)], axis=-1).transpose(1, 0, 2)
    pad = ((0, 0), (GDN_GROUP_V, LANE - 2 * GDN_GROUP_V))
    gate_par = jnp.stack([jnp.pad(a_log.reshape(ng, GDN_GROUP_V), pad),
                          jnp.pad(dt_bias.reshape(ng, GDN_GROUP_V), pad)], axis=1)
    o, s_p = gdn_core(proj, gates, conv_w, gate_par, norm_g, None, None, None,
                      row0=0, nb=bp, t=tp, c=min(GDN_CHUNK, tp), n_v_heads=nv)
    o, s_s = gdn_core(proj, gates, conv_w, gate_par, norm_g, conv_buf, s0, o,
                      row0=mp, nb=bs, t=ts, c=ts, n_v_heads=nv)
    assert tp >= CONV_WIDTH - 1 and ts >= CONV_WIDTH - 1
    conv_p = proj[:mp, :conv_dim].reshape(bp, tp, conv_dim)[:, tp - (CONV_WIDTH - 1):]
    conv_s = proj[mp:, :conv_dim].reshape(bs, ts, conv_dim)[:, ts - (CONV_WIDTH - 1):]
    x = mm_res(o, w_out.astype(BF16), x, tm=MM_TM, tn=MM_TN, tk=MM_TK)
    return x, (conv_p, s_p), (conv_s, s_s)


def _fox_layer(x, g, w_in, b_f, w_out, k_past, v_past, lf_past, dims):
    bp, tp, bs, ts = dims
    mp = bp * tp
    nh = b_f.shape[0]
    hd = FOX_HEAD_DIM
    fd = nh * hd
    npast = k_past.shape[1]
    w_main, w_tail = _split_w_in(w_in, 3 * fd)
    proj, tail = rms_mm(x, g, w_main, w_tail)
    bias = _lanes(b_f, 0)
    lf_p, c_p = fox_gate(tail, bias, None, row0=0, nb=bp, t=tp)
    c_past = fox_gate(lf_past.reshape(bs * npast, nh), None, None, row0=0, nb=bs, t=npast)
    c_past = c_past.reshape(bs, npast, nh)
    init = jnp.pad(c_past[:, npast - 1:, :], ((0, 0), (0, 0), (0, LANE - nh)))
    lf_s, c_s = fox_gate(tail, bias, init, row0=mp, nb=bs, t=ts)

    def layouts(c, b, t):
        ct = c[:, :nh].reshape(b, t, nh).transpose(0, 2, 1)
        return jnp.broadcast_to(ct[..., None], (b, nh, t, LANE)), ct[:, :, None, :]

    cq_p, ck_p = layouts(c_p, bp, tp)
    cq_s, ck_s = layouts(c_s, bs, ts)
    ck_past = c_past.transpose(0, 2, 1)[:, :, None, :]
    blk = min(FOX_BLOCK, tp)
    o = fox_flash(proj, cq_p, ck_p, nb=bp, t=tp, n_heads=nh, bq=blk, bk=blk)
    o = fox_sample(proj, k_past, v_past, cq_s, ck_past, ck_s, o, row0=mp, nb=bs, t=ts, n_heads=nh)
    x = mm_res(o, w_out.astype(BF16), x, tm=MM_TM, tn=MM_TN, tk=MM_TK)

    def outs(rows, lf, b, t):
        return (rows[:, fd:2 * fd].reshape(b, t, nh, hd), rows[:, 2 * fd:3 * fd].reshape(b, t, nh, hd),
                lf[:, :nh].reshape(b, t, nh))

    return x, outs(proj[:mp], lf_p, bp, tp), outs(proj[mp:], lf_s, bs, ts)


def _gla_layer(x, g, w_in, w_gate, b_gate, norm_g, w_out, s0, dims):
    bp, tp, bs, ts = dims
    mp = bp * tp
    nh = s0.shape[1]
    kd = w_gate.shape[1]
    w_main, w_tail = _split_w_in(w_in, 6 * kd)
    proj, glr = rms_mm(x, g, w_main, w_tail)
    wg = jnp.pad(w_gate, ((0, LANE - w_gate.shape[0]), (0, 0))).astype(BF16)
    o, s_p = gla_core(proj, glr, wg, b_gate, norm_g, None, None,
                      row0=0, nb=bp, t=tp, c=min(GLA_CHUNK, tp), n_heads=nh)
    o, s_s = gla_core(proj, glr, wg, b_gate, norm_g, s0, o, row0=mp, nb=bs, t=ts, c=ts, n_heads=nh)
    x = mm_res(o, w_out.astype(BF16), x, tm=MM_TM, tn=MM_TN, tk=MM_TK)
    return x, s_p, s_s


def kernel(x_prompt, x_sample, mem_prompt, state_gdn_conv, state_gdn_s, cache_fox_k, cache_fox_v, cache_fox_logf, state_gla_s, cache_mem_k, cache_mem_v, norm_mix, norm_mem, norm_memsrc, norm_ffn, norm_final, gdn_w_in, gdn_conv_w, gdn_a_log, gdn_dt_bias, gdn_norm, gdn_w_out, fox_w_in, fox_b_f, fox_w_out, gla_w_in, gla_w_gate, gla_b_gate, gla_norm, gla_w_out, mem_w_q, mem_w_kv, mem_w_o, ffn_w_up, ffn_w_down):
    bp, tp, d = x_prompt.shape
    bs, ts, _ = x_sample.shape
    dims = (bp, tp, bs, ts)
    mp, ms = bp * tp, bs * ts
    depth = norm_mix.shape[0]
    mem_len = mem_prompt.shape[1]
    mem_dim = mem_w_q.shape[2]
    x = jnp.concatenate([x_prompt.reshape(mp, d), x_sample.reshape(ms, d)], axis=0)
    mem = mem_prompt.reshape(bp * mem_len, d)

    gdn_p, gdn_s, fox_p, fox_s, gla_p, gla_s, kv_p = [], [], [], [], [], [], []
    for i in range(depth):
        j, kind = divmod(i, 3)
        if kind == 0:
            x, res_p, res_s = _gdn_layer(x, norm_mix[i], gdn_w_in[j], gdn_conv_w[j], gdn_a_log[j], gdn_dt_bias[j],
                                         gdn_norm[j], gdn_w_out[j], state_gdn_conv[j], state_gdn_s[j], dims)
            gdn_p.append(res_p)
            gdn_s.append(res_s)
        elif kind == 1:
            x, res_p, res_s = _fox_layer(x, norm_mix[i], fox_w_in[j], fox_b_f[j], fox_w_out[j],
                                         cache_fox_k[j], cache_fox_v[j], cache_fox_logf[j], dims)
            fox_p.append(res_p)
            fox_s.append(res_s)
        else:
            x, res_p, res_s = _gla_layer(x, norm_mix[i], gla_w_in[j], gla_w_gate[j], gla_b_gate[j], gla_norm[j],
                                         gla_w_out[j], state_gla_s[j], dims)
            gla_p.append(res_p)
            gla_s.append(res_s)

        kv = rms_mm(mem, norm_memsrc[i], mem_w_kv[i].astype(BF16)).reshape(bp, mem_len, 2 * mem_dim)
        kv_p.append(kv)
        wq = mem_w_q[i].astype(BF16)
        wo = mem_w_o[i].astype(BF16)
        x = mem_attn(x, norm_mem[i], wq, kv, kv, wo, row0=0, nb=bp, t=tp, tq=min(256, tp),
                     kmap=lambda b, q: (b, 0, 0), vmap=lambda b, q: (b, 0, 1), kblk=mem_dim)
        x = mem_attn(x, norm_mem[i], wq, cache_mem_k[i].reshape(bs, mem_len, mem_dim),
                     cache_mem_v[i].reshape(bs, mem_len, mem_dim), wo, row0=mp, nb=bs, t=ts, tq=ts,
                     kmap=lambda b, q: (b, 0, 0), vmap=lambda b, q: (b, 0, 0), kblk=mem_dim)

        u = rms_mm(x, norm_ffn[i], ffn_w_up[i].astype(BF16), act="relu2", out_dtype=BF16)
        x = mm_res(u, ffn_w_down[i].astype(BF16), x, tm=MM_TM, tn=MM_TN, tk=MM_TK)

    y = rmsnorm(x, norm_final)
    kv_all = jnp.stack(kv_p)
    mem_shape = (depth, bp, mem_len, MEM_HEADS, MEM_HEAD_DIM)
    stack = lambda items, k: jnp.stack([it[k] for it in items])
    return (
        y[:mp].reshape(bp, tp, d), y[mp:].reshape(bs, ts, d),
        stack(gdn_p, 0), stack(gdn_p, 1), stack(fox_p, 0), stack(fox_p, 1), stack(fox_p, 2), jnp.stack(gla_p),
        kv_all[..., :mem_dim].reshape(mem_shape), kv_all[..., mem_dim:].reshape(mem_shape),
        stack(gdn_s, 0), stack(gdn_s, 1), stack(fox_s, 0), stack(fox_s, 1), stack(fox_s, 2), jnp.stack(gla_s),
    )
```

```python
import functools
import math

import jax
import jax.numpy as jnp
from jax import lax
from jax.experimental import pallas as pl
from jax.experimental.pallas import tpu as pltpu

F32 = jnp.float32
BF16 = jnp.bfloat16
NORM_EPS = 1e-6
LANE = 128
CONV_WIDTH = 4
GDN_HEAD_DIM = 128
FOX_HEAD_DIM = 128
FOX_HEADS = 32
GLA_HEADS = 4
GLA_TAU = 16.0
GLA_SUB = 16
MEM_HEADS = 4
MEM_HEAD_DIM = 128
VMEM_LIMIT = 56 * 1024 * 1024

_NT = (((1,), (1,)), ((), ()))
_TN = (((0,), (0,)), ((), ()))


def _params(*sem):
    return pltpu.CompilerParams(dimension_semantics=sem, vmem_limit_bytes=VMEM_LIMIT)


def _pick(n, pref):
    t = pref
    while n % t:
        assert t % 2 == 0 and t > 8, (n, pref)
        t //= 2
    return t


def _dot(a, b, dims=None):
    a = a.astype(BF16)
    b = b.astype(BF16)
    if dims is None:
        return jnp.dot(a, b, preferred_element_type=F32)
    return lax.dot_general(a, b, dims, preferred_element_type=F32)


def _split3(x):
    hi = x.astype(BF16)
    r = x - hi.astype(F32)
    mid = r.astype(BF16)
    lo = (r - mid.astype(F32)).astype(BF16)
    return hi, mid, lo


def _cumsum_rows(x):
    n = x.shape[0]
    tri = (lax.broadcasted_iota(jnp.int32, (n, n), 1) <= lax.broadcasted_iota(jnp.int32, (n, n), 0))
    tri = jnp.where(tri, 1.0, 0.0).astype(BF16)
    hi, mid, lo = _split3(x)
    out = jnp.dot(tri, lo, preferred_element_type=F32)
    out = out + jnp.dot(tri, mid, preferred_element_type=F32)
    return out + jnp.dot(tri, hi, preferred_element_type=F32)


def _log_sigmoid(x):
    return jnp.minimum(x, 0.0) - jnp.log1p(jnp.exp(-jnp.abs(x)))


def _softplus(x):
    return jnp.maximum(x, 0.0) + jnp.log1p(jnp.exp(-jnp.abs(x)))


def _sigmoid(x):
    return 1.0 / (1.0 + jnp.exp(-x))


def _silu(x):
    return x * _sigmoid(x)


def _rms_rows(x, g):
    return x * lax.rsqrt(jnp.mean(x * x, axis=-1, keepdims=True) + NORM_EPS) * g


def _rms_mm_kernel(*refs, act, has_tail):
    if has_tail:
        x_ref, g_ref, w_ref, wt_ref, o_ref, ot_ref, h_ref = refs
    else:
        x_ref, g_ref, w_ref, o_ref, h_ref = refs

    @pl.when(pl.program_id(1) == 0)
    def _():
        h_ref[...] = _rms_rows(x_ref[...], g_ref[...]).astype(BF16)
        if has_tail:
            ot_ref[...] = jnp.dot(h_ref[...], wt_ref[...], preferred_element_type=F32)

    acc = jnp.dot(h_ref[...], w_ref[...], preferred_element_type=F32)
    if act == "relu2":
        acc = jnp.square(jnp.maximum(acc, 0.0))
    o_ref[...] = acc.astype(o_ref.dtype)


def rms_mm(x, g, w, w_tail=None, *, act=None, out_dtype=F32, tm=512, tn=1024):
    m, k = x.shape
    n = w.shape[1]
    tm, tn = _pick(m, tm), _pick(n, tn)
    has_tail = w_tail is not None
    in_specs = [
        pl.BlockSpec((tm, k), lambda i, j: (i, 0)),
        pl.BlockSpec((1, k), lambda i, j: (0, 0)),
        pl.BlockSpec((k, tn), lambda i, j: (0, j)),
    ]
    out_specs = [pl.BlockSpec((tm, tn), lambda i, j: (i, j))]
    out_shape = [jax.ShapeDtypeStruct((m, n), out_dtype)]
    args = [x, g.reshape(1, k), w]
    if has_tail:
        in_specs.append(pl.BlockSpec((k, LANE), lambda i, j: (0, 0)))
        out_specs.append(pl.BlockSpec((tm, LANE), lambda i, j: (i, 0)))
        out_shape.append(jax.ShapeDtypeStruct((m, LANE), F32))
        args.append(w_tail)
    res = pl.pallas_call(
        functools.partial(_rms_mm_kernel, act=act, has_tail=has_tail),
        grid=(m // tm, n // tn),
        in_specs=in_specs,
        out_specs=out_specs,
        out_shape=out_shape,
        scratch_shapes=[pltpu.VMEM((tm, k), BF16)],
        compiler_params=_params("parallel", "arbitrary"),
        name="rms_mm",
    )(*args)
    return res if has_tail else res[0]


def _mm_res_kernel(a_ref, w_ref, r_ref, o_ref):
    @pl.when(pl.program_id(2) == 0)
    def _():
        o_ref[...] = r_ref[...]

    o_ref[...] += jnp.dot(a_ref[...], w_ref[...], preferred_element_type=F32)


def mm_res(a, w, res, *, tm=512, tn=512, tk=None):
    m, k = a.shape
    n = w.shape[1]
    tm, tn, tk = _pick(m, tm), _pick(n, tn), _pick(k, k if tk is None else tk)
    return pl.pallas_call(
        _mm_res_kernel,
        grid=(m // tm, n // tn, k // tk),
        in_specs=[
            pl.BlockSpec((tm, tk), lambda i, j, l: (i, l)),
            pl.BlockSpec((tk, tn), lambda i, j, l: (l, j)),
            pl.BlockSpec((tm, tn), lambda i, j, l: (i, j)),
        ],
        out_specs=pl.BlockSpec((tm, tn), lambda i, j, l: (i, j)),
        out_shape=jax.ShapeDtypeStruct((m, n), F32),
        input_output_aliases={2: 0},
        compiler_params=_params("parallel", "parallel", "arbitrary"),
        name="mm_res",
    )(a, w, res)


def _rmsnorm_split_kernel(x_ref, g_ref, a_ref, b_ref, *, na):
    y = _rms_rows(x_ref[...], g_ref[...])

    @pl.when(pl.program_id(0) < na)
    def _():
        a_ref[...] = y

    @pl.when(pl.program_id(0) >= na)
    def _():
        b_ref[...] = y


def rmsnorm_split(x, g, ma, *, tm=512):
    m, k = x.shape
    mb = m - ma
    tm = _pick(math.gcd(ma, mb), tm)
    na = ma // tm
    return pl.pallas_call(
        functools.partial(_rmsnorm_split_kernel, na=na),
        grid=(m // tm,),
        in_specs=[pl.BlockSpec((tm, k), lambda i: (i, 0)), pl.BlockSpec((1, k), lambda i: (0, 0))],
        out_specs=[pl.BlockSpec((tm, k), lambda i: (jnp.minimum(i, na - 1), 0)),
                   pl.BlockSpec((tm, k), lambda i: (jnp.maximum(i - na, 0), 0))],
        out_shape=[jax.ShapeDtypeStruct((ma, k), F32), jax.ShapeDtypeStruct((mb, k), F32)],
        compiler_params=_params("arbitrary"),
        name="rmsnorm_split",
    )(x, g.reshape(1, k))


def _mem_attn_kernel(x_ref, g_ref, wq_ref, k_ref, v_ref, wo_ref, o_ref):
    x = x_ref[...]
    h = _rms_rows(x, g_ref[...]).astype(BF16)
    q = jnp.dot(h, wq_ref[...], preferred_element_type=F32)
    k = k_ref[...].astype(BF16)
    v = v_ref[...].astype(BF16)
    outs = []
    for hh in range(MEM_HEADS):
        sl = slice(hh * MEM_HEAD_DIM, (hh + 1) * MEM_HEAD_DIM)
        s = _dot(q[:, sl], k[:, sl], _NT) * MEM_HEAD_DIM ** -0.5
        p = jnp.exp(s - jnp.max(s, axis=-1, keepdims=True))
        p = p / jnp.sum(p, axis=-1, keepdims=True)
        outs.append(_dot(p, v[:, sl]))
    o = jnp.concatenate(outs, axis=1).astype(BF16)
    o_ref[...] = x + jnp.dot(o, wo_ref[...], preferred_element_type=F32)


def mem_attn(x, g, wq, k, v, wo, *, row0, nb, t, tq, kmap, vmap, kblk):
    m, d = x.shape
    nq = t // tq
    r0 = row0 // tq
    dm = wq.shape[1]
    mem_len = k.shape[1]
    xmap = lambda b, i: (r0 + b * nq + i, 0)
    const = lambda b, i: (0, 0)
    return pl.pallas_call(
        _mem_attn_kernel,
        grid=(nb, nq),
        in_specs=[
            pl.BlockSpec((tq, d), xmap),
            pl.BlockSpec((1, d), const),
            pl.BlockSpec((d, dm), const),
            pl.BlockSpec((None, mem_len, kblk), kmap),
            pl.BlockSpec((None, mem_len, kblk), vmap),
            pl.BlockSpec((dm, d), const),
        ],
        out_specs=pl.BlockSpec((tq, d), xmap),
        out_shape=jax.ShapeDtypeStruct((m, d), F32),
        input_output_aliases={0: 0},
        compiler_params=_params("parallel", "parallel"),
        name="mem_attn",
    )(x, g.reshape(1, d), wq, k, v, wo)


def _fox_gate_kernel(*refs, gate, has_init):
    refs = list(refs)
    x_ref = refs.pop(0)
    bias_ref = refs.pop(0) if gate else None
    init_ref = refs.pop(0) if has_init else None
    lf_ref = refs.pop(0) if gate else None
    c_ref, carry_ref = refs

    @pl.when(pl.program_id(1) == 0)
    def _():
        carry_ref[...] = init_ref[...] if has_init else jnp.zeros_like(carry_ref)

    x = x_ref[...]
    if gate:
        x = _log_sigmoid(x + bias_ref[...])
        lf_ref[...] = x
    c = _cumsum_rows(x) + carry_ref[...]
    c_ref[...] = c
    carry_ref[...] = c[-1:, :]


def fox_gate(x, bias, init, *, row0, nb, t):
    w = x.shape[1]
    tb = min(t, 256)
    nt = t // tb
    r0 = row0 // tb
    gate = bias is not None
    has_init = init is not None
    xmap = lambda b, i: (r0 + b * nt + i, 0)
    omap = lambda b, i: (b * nt + i, 0)
    in_specs = [pl.BlockSpec((tb, w), xmap)]
    args = [x]
    if gate:
        in_specs.append(pl.BlockSpec((1, w), lambda b, i: (0, 0)))
        args.append(bias)
    if has_init:
        in_specs.append(pl.BlockSpec((None, 1, w), lambda b, i: (b, 0, 0)))
        args.append(init)
    n_out = 2 if gate else 1
    res = pl.pallas_call(
        functools.partial(_fox_gate_kernel, gate=gate, has_init=has_init),
        grid=(nb, nt),
        in_specs=in_specs,
        out_specs=[pl.BlockSpec((tb, w), omap)] * n_out,
        out_shape=[jax.ShapeDtypeStruct((nb * t, w), F32)] * n_out,
        scratch_shapes=[pltpu.VMEM((1, w), F32)],
        compiler_params=_params("parallel", "arbitrary"),
        name="fox_gate",
    )(*args)
    return res if gate else res[0]


def _fox_flash_kernel(q_ref, k_ref, v_ref, cq_ref, ck_ref, o_ref, *, blk):
    t = q_ref.shape[0]
    nq = t // blk
    q_b = q_ref[...].astype(BF16)
    k_b = k_ref[...].astype(BF16)
    v_b = v_ref[...].astype(BF16)
    cq = cq_ref[...]
    ck = ck_ref[...]
    row = lax.broadcasted_iota(jnp.int32, (blk, blk), 0)
    col = lax.broadcasted_iota(jnp.int32, (blk, blk), 1)
    m, l, acc = [None] * nq, [None] * nq, [None] * nq
    for j in range(nq):
        ks = slice(j * blk, (j + 1) * blk)
        for i in range(j, nq):
            qs = slice(i * blk, (i + 1) * blk)
            s = lax.dot_general(q_b[qs], k_b[ks], _NT, preferred_element_type=F32) * FOX_HEAD_DIM ** -0.5
            s = s + (jnp.concatenate([cq[qs]] * (blk // LANE), axis=1) - ck[:, ks])
            if i == j:
                s = jnp.where(col <= row, s, -jnp.inf)
            s_max = jnp.max(s, axis=-1, keepdims=True)
            if j == 0:
                m[i] = s_max
                p = jnp.exp(s - s_max)
                l[i] = jnp.sum(p, axis=-1, keepdims=True)
                acc[i] = jnp.dot(p.astype(BF16), v_b[ks], preferred_element_type=F32)
            else:
                m_new = jnp.maximum(m[i], s_max)
                alpha = jnp.exp(m[i] - m_new)
                p = jnp.exp(s - m_new)
                l[i] = alpha * l[i] + jnp.sum(p, axis=-1, keepdims=True)
                acc[i] = alpha * acc[i] + jnp.dot(p.astype(BF16), v_b[ks], preferred_element_type=F32)
                m[i] = m_new
    o_ref[...] = jnp.concatenate([acc[i] / l[i] for i in range(nq)], axis=0).astype(o_ref.dtype)


def fox_flash(proj, cq_b, ck_row, *, nb, t, n_heads, blk=512):
    assert t % blk == 0 and blk % LANE == 0
    d = FOX_HEAD_DIM
    return pl.pallas_call(
        functools.partial(_fox_flash_kernel, blk=blk),
        grid=(nb, n_heads),
        in_specs=[
            pl.BlockSpec((t, d), lambda b, h: (b, h)),
            pl.BlockSpec((t, d), lambda b, h: (b, n_heads + h)),
            pl.BlockSpec((t, d), lambda b, h: (b, 2 * n_heads + h)),
            pl.BlockSpec((None, None, t, LANE), lambda b, h: (b, h, 0, 0)),
            pl.BlockSpec((None, None, 1, t), lambda b, h: (b, h, 0, 0)),
        ],
        out_specs=pl.BlockSpec((t, d), lambda b, h: (b, h)),
        out_shape=jax.ShapeDtypeStruct((proj.shape[0], n_heads * d), BF16),
        compiler_params=_params("parallel", "parallel"),
        name="fox_flash",
    )(proj, proj, proj, cq_b, ck_row)


FOX_SAMPLE_GROUP = 8


def _fox_sample_kernel(q_ref, kn_ref, vn_ref, kp_ref, vp_ref, cq_ref, ckp_ref, ckn_ref, prev_ref,
                       o_ref, ko_ref, vo_ref):
    del prev_ref
    t = q_ref.shape[0]
    d = FOX_HEAD_DIM
    scale = d ** -0.5
    npast = kp_ref.shape[0]
    row = lax.broadcasted_iota(jnp.int32, (t, t), 0)
    col = lax.broadcasted_iota(jnp.int32, (t, t), 1)
    outs = []
    for hh in range(FOX_SAMPLE_GROUP):
        sl = slice(hh * d, (hh + 1) * d)
        q = q_ref[:, sl]
        kn = kn_ref[:, sl]
        vn = vn_ref[:, sl]
        ko_ref[:, hh, :] = kn
        vo_ref[:, hh, :] = vn
        cq = cq_ref[hh]
        cq_wide = jnp.concatenate([cq] * (npast // LANE), axis=1)
        s_p = _dot(q, kp_ref[:, sl], _NT) * scale + (cq_wide - ckp_ref[hh])
        s_n = _dot(q, kn, _NT) * scale + (cq[:, :t] - ckn_ref[hh])
        s_n = jnp.where(col <= row, s_n, -jnp.inf)
        m = jnp.maximum(jnp.max(s_p, axis=-1, keepdims=True), jnp.max(s_n, axis=-1, keepdims=True))
        p_p = jnp.exp(s_p - m)
        p_n = jnp.exp(s_n - m)
        inv = 1.0 / (jnp.sum(p_p, axis=-1, keepdims=True) + jnp.sum(p_n, axis=-1, keepdims=True))
        outs.append(_dot(p_p * inv, vp_ref[:, sl]) + _dot(p_n * inv, vn))
    o_ref[...] = jnp.concatenate(outs, axis=1).astype(o_ref.dtype)


def fox_sample(proj, k_past, v_past, cq_b, ck_past, ck_new, o_prev, *, row0, nb, t, n_heads):
    d = FOX_HEAD_DIM
    hg = FOX_SAMPLE_GROUP
    ng = n_heads // hg
    r0 = row0 // t
    npast = k_past.shape[1]
    past = lambda b, g: (b, 0, g)
    bias = lambda b, g: (b, g, 0, 0)
    return pl.pallas_call(
        _fox_sample_kernel,
        grid=(nb, ng),
        in_specs=[
            pl.BlockSpec((t, hg * d), lambda b, g: (r0 + b, g)),
            pl.BlockSpec((t, hg * d), lambda b, g: (r0 + b, ng + g)),
            pl.BlockSpec((t, hg * d), lambda b, g: (r0 + b, 2 * ng + g)),
            pl.BlockSpec((None, npast, hg * d), past),
            pl.BlockSpec((None, npast, hg * d), past),
            pl.BlockSpec((None, hg, t, LANE), bias),
            pl.BlockSpec((None, hg, 1, npast), bias),
            pl.BlockSpec((None, hg, 1, t), bias),
            pl.BlockSpec(memory_space=pl.ANY),
        ],
        out_specs=[pl.BlockSpec((t, hg * d), lambda b, g: (r0 + b, g)),
                   pl.BlockSpec((t, hg, d), lambda b, g: (b, g, 0)),
                   pl.BlockSpec((t, hg, d), lambda b, g: (b, g, 0))],
        out_shape=[
            jax.ShapeDtypeStruct(o_prev.shape, o_prev.dtype),
            jax.ShapeDtypeStruct((nb * t, n_heads, d), F32),
            jax.ShapeDtypeStruct((nb * t, n_heads, d), F32),
        ],
        input_output_aliases={8: 0},
        compiler_params=_params("parallel", "parallel"),
        name="fox_sample",
    )(proj, proj, proj, k_past, v_past, cq_b, ck_past, ck_new, o_prev)


GDN_GROUP_V = 8
GDN_INV_BASE = 16


def _transpose_lanes(x):
    eye = (lax.broadcasted_iota(jnp.int32, (LANE, LANE), 0) == lax.broadcasted_iota(jnp.int32, (LANE, LANE), 1))
    eye = jnp.where(eye, 1.0, 0.0).astype(BF16)
    hi, mid, lo = _split3(x)
    out = lax.dot_general(eye, lo, _NT, preferred_element_type=F32)
    out = out + lax.dot_general(eye, mid, _NT, preferred_element_type=F32)
    return out + lax.dot_general(eye, hi, _NT, preferred_element_type=F32)


def _unit_lower_inverses(mats):
    n = mats[0].shape[0]
    row = lax.broadcasted_iota(jnp.int32, (n, n), 0)
    col = lax.broadcasted_iota(jnp.int32, (n, n), 1)
    blk = min(GDN_INV_BASE, n)
    shift = int(math.log2(blk))
    same = (row >> shift) == (col >> shift)
    eye = jnp.where(row == col, 1.0, 0.0).astype(F32)
    ps = [jnp.where(same, a, 0.0).astype(BF16) for a in mats]
    ts = [eye - p.astype(F32) for p in ps]
    for _ in range(shift - 1):
        ps = [jnp.dot(p, p, preferred_element_type=F32).astype(BF16) for p in ps]
        ts = [t + jnp.dot(t.astype(BF16), p, preferred_element_type=F32) for t, p in zip(ts, ps)]
    while blk < n:
        blk *= 2
        shift += 1
        same2 = (row >> shift) == (col >> shift)
        off = jnp.logical_and(same2, jnp.logical_not(same))
        ms = [jnp.where(off, a, 0.0).astype(BF16) for a in mats]
        tbs = [t.astype(BF16) for t in ts]
        tms = [jnp.dot(tb, m, preferred_element_type=F32).astype(BF16) for tb, m in zip(tbs, ms)]
        ts = [t - jnp.dot(tm, tb, preferred_element_type=F32) for t, tm, tb in zip(ts, tms, tbs)]
        same = same2
    return ts


def _gdn_kernel(*refs, c, has_state):
    refs = list(refs)
    q_ref, k_ref, v_ref, z_ref, gate_ref, wq_ref, wk_ref, wv_ref, gp_ref, ng_ref = refs[:10]
    refs = refs[10:]
    if has_state:
        bq_ref, bk_ref, bv_ref, s0_ref, _prev = refs[:5]
        refs = refs[5:]
    o_ref, s_ref, xq_ref, xk_ref, xv_ref = refs
    d = GDN_HEAD_DIM

    @pl.when(pl.program_id(2) == 0)
    def _():
        for x_ref, b_ref in ((xq_ref, bq_ref if has_state else None), (xk_ref, bk_ref if has_state else None),
                             (xv_ref, bv_ref if has_state else None)):
            x_ref[0:8, :] = jnp.zeros((8, x_ref.shape[1]), F32)
            if has_state:
                x_ref[8 - (CONV_WIDTH - 1):8, :] = b_ref[...]
        s_ref[...] = s0_ref[...] if has_state else jnp.zeros_like(s_ref)

    def conv(x_ref, raw_ref, w_ref):
        x_ref[8:8 + c, :] = raw_ref[...]
        w = w_ref[...]
        out = x_ref[8:8 + c, :] * w[3:4, :]
        for i in range(CONV_WIDTH - 1):
            out = out + x_ref[5 + i:5 + i + c, :] * w[i:i + 1, :]
        x_ref[0:8, :] = x_ref[c:c + 8, :]
        return _silu(out)

    qc = conv(xq_ref, q_ref, wq_ref)
    kc = conv(xk_ref, k_ref, wk_ref)
    vc = conv(xv_ref, v_ref, wv_ref)

    gt = gate_ref[...]
    gp = gp_ref[...]
    beta_all = _sigmoid(gt)
    g_all = -jnp.exp(gp[0:1, :]) * _softplus(gt + gp[1:2, :])
    gc_all = _cumsum_rows(g_all)
    gc_t = _transpose_lanes(gc_all)

    row = lax.broadcasted_iota(jnp.int32, (c, c), 0)
    col = lax.broadcasted_iota(jnp.int32, (c, c), 1)
    incl = col <= row
    strict = col < row
    ng = ng_ref[...]
    z = z_ref[...]

    heads = range(GDN_GROUP_V)
    q_hs, k_hs, kks, qks = [], [], [], []
    for hq in range(GDN_GROUP_V // 2):
        sl = slice(hq * d, (hq + 1) * d)
        q_h = qc[:, sl]
        k_h = kc[:, sl]
        q_h = q_h * lax.rsqrt(jnp.sum(q_h * q_h, axis=-1, keepdims=True) + NORM_EPS) * d ** -0.5
        k_h = k_h * lax.rsqrt(jnp.sum(k_h * k_h, axis=-1, keepdims=True) + NORM_EPS)
        k_b = k_h.astype(BF16)
        q_hs.append(q_h)
        k_hs.append(k_h)
        kks.append(lax.dot_general(k_b, k_b, _NT, preferred_element_type=F32))
        qks.append(lax.dot_general(q_h.astype(BF16), k_b, _NT, preferred_element_type=F32))

    betas = [beta_all[:, j:j + 1] for j in heads]
    gcols = [gc_all[:, 8 + j:9 + j] for j in heads]
    glasts = [gc_all[c - 1:c, 8 + j:9 + j] for j in heads]
    egs = [jnp.exp(gcols[j]) for j in heads]
    decays = [jnp.exp(jnp.where(incl, gcols[j] - gc_t[8 + j:9 + j, :], -jnp.inf)) for j in heads]
    tinvs = _unit_lower_inverses([jnp.where(strict, betas[j] * kks[j // 2] * decays[j], 0.0) for j in heads])
    rhss = [jnp.concatenate([k_hs[j // 2] * (betas[j] * egs[j]), vc[:, j * d:(j + 1) * d] * betas[j]], axis=1)
            for j in heads]
    sols = [_dot(tinvs[j], rhss[j]) for j in heads]
    s_olds = [s_ref[j] for j in heads]
    wss = [_dot(jnp.concatenate([sols[j][:, :d], q_hs[j // 2] * egs[j]], axis=0), s_olds[j]) for j in heads]
    u_news = [sols[j][:, d:] - wss[j][:c] for j in heads]
    os_ = [wss[j][c:] + _dot(qks[j // 2] * decays[j], u_news[j]) for j in heads]
    s_news = [s_olds[j] * jnp.exp(glasts[j])
              + _dot(k_hs[j // 2] * jnp.exp(glasts[j] - gcols[j]), u_news[j], _TN) for j in heads]
    s_ref[...] = jnp.stack(s_news, axis=0)
    outs = [o * lax.rsqrt(jnp.mean(o * o, axis=-1, keepdims=True) + NORM_EPS) * ng for o in os_]
    o_ref[...] = (jnp.concatenate(outs, axis=1) * _silu(z)).astype(o_ref.dtype)


def gdn_core(proj, gates, conv_w, gate_par, norm_g, conv_buf, s0, o_prev, *, row0, nb, t, c, n_v_heads):
    has_state = s0 is not None
    d = GDN_HEAD_DIM
    gv = GDN_GROUP_V
    ng = n_v_heads // gv
    wqk = (gv // 2) * d
    wv = gv * d
    nqk = (n_v_heads // 2) * d // wqk
    nc = t // c
    r0 = row0 // c
    m = proj.shape[0]
    rows = lambda b, g, i: r0 + b * nc + i
    in_specs = [
        pl.BlockSpec((c, wqk), lambda b, g, i: (rows(b, g, i), g)),
        pl.BlockSpec((c, wqk), lambda b, g, i: (rows(b, g, i), nqk + g)),
        pl.BlockSpec((c, wv), lambda b, g, i: (rows(b, g, i), ng + g)),
        pl.BlockSpec((c, wv), lambda b, g, i: (rows(b, g, i), 2 * ng + g)),
        pl.BlockSpec((None, c, LANE), lambda b, g, i: (g, rows(b, g, i), 0)),
        pl.BlockSpec((CONV_WIDTH, wqk), lambda b, g, i: (0, g)),
        pl.BlockSpec((CONV_WIDTH, wqk), lambda b, g, i: (0, nqk + g)),
        pl.BlockSpec((CONV_WIDTH, wv), lambda b, g, i: (0, ng + g)),
        pl.BlockSpec((None, 2, LANE), lambda b, g, i: (g, 0, 0)),
        pl.BlockSpec((1, d), lambda b, g, i: (0, 0)),
    ]
    args = [proj, proj, proj, proj, gates, conv_w, conv_w, conv_w, gate_par, norm_g.reshape(1, d)]
    aliases = {}
    if has_state:
        in_specs += [
            pl.BlockSpec((None, CONV_WIDTH - 1, wqk), lambda b, g, i: (b, 0, g)),
            pl.BlockSpec((None, CONV_WIDTH - 1, wqk), lambda b, g, i: (b, 0, nqk + g)),
            pl.BlockSpec((None, CONV_WIDTH - 1, wv), lambda b, g, i: (b, 0, ng + g)),
            pl.BlockSpec((None, gv, d, d), lambda b, g, i: (b, g, 0, 0)),
            pl.BlockSpec(memory_space=pl.ANY),
        ]
        args += [conv_buf, conv_buf, conv_buf, s0, o_prev]
        aliases = {14: 0}
    return pl.pallas_call(
        functools.partial(_gdn_kernel, c=c, has_state=has_state),
        grid=(nb, ng, nc),
        in_specs=in_specs,
        out_specs=[
            pl.BlockSpec((c, wv), lambda b, g, i: (rows(b, g, i), g)),
            pl.BlockSpec((None, gv, d, d), lambda b, g, i: (b, g, 0, 0)),
        ],
        out_shape=[
            jax.ShapeDtypeStruct((m, n_v_heads * d), BF16),
            jax.ShapeDtypeStruct((nb, n_v_heads, d, d), F32),
        ],
        scratch_shapes=[pltpu.VMEM((c + 8, wqk), F32), pltpu.VMEM((c + 8, wqk), F32), pltpu.VMEM((c + 8, wv), F32)],
        input_output_aliases=aliases,
        compiler_params=_params("parallel", "parallel", "arbitrary"),
        name="gdn_core",
    )(*args)


def _gla_kernel(*refs, c, has_state):
    refs = list(refs)
    q_ref, k_ref, v_ref, r_ref, glr_ref, wg_ref, bg_ref, ng_ref = refs[:8]
    refs = refs[8:]
    if has_state:
        s0_ref, _prev = refs[:2]
        refs = refs[2:]
    o_ref, s_ref, st_ref = refs
    dk = q_ref.shape[1]
    sub = min(GLA_SUB, c)
    ci = pl.program_id(2)

    @pl.when(ci == 0)
    def _():
        st_ref[...] = s0_ref[...].T if has_state else jnp.zeros_like(st_ref)

    glog = _log_sigmoid(_dot(glr_ref[...], wg_ref[...]) + bg_ref[...]) * (1.0 / GLA_TAU)
    bc = _cumsum_rows(glog)
    q = q_ref[...] * dk ** -0.5
    k = k_ref[...]
    v = v_ref[...].astype(BF16)
    st = st_ref[...]
    o = _dot(q * jnp.exp(bc), st, _NT)

    tok = lax.broadcasted_iota(jnp.int32, (c, 1), 0)
    a_rows = []
    for i in range(c // sub):
        lo = i * sub
        if i == 0:
            a_rows.append(jnp.zeros((sub, c), F32))
            continue
        ref_i = bc[lo:lo + 1, :]
        q_off = q[lo:lo + sub, :] * jnp.exp(bc[lo:lo + sub, :] - ref_i)
        k_off = k * jnp.exp(jnp.where(tok < lo, ref_i - bc, -jnp.inf))
        a_rows.append(_dot(q_off, k_off, _NT))
    a = jnp.concatenate(a_rows, axis=0)

    row = lax.broadcasted_iota(jnp.int32, (c, c), 0)
    col = lax.broadcasted_iota(jnp.int32, (c, c), 1)
    rmod = jnp.bitwise_and(row, sub - 1)
    for dlt in range(sub):
        k_r = k if dlt == 0 else pltpu.roll(k, dlt, 0)
        b_r = bc if dlt == 0 else pltpu.roll(bc, dlt, 0)
        e = jnp.exp(jnp.minimum(bc - b_r, 0.0))
        val = jnp.sum(q * k_r * e, axis=-1, keepdims=True)
        a = a + jnp.where(jnp.logical_and(col == row - dlt, rmod >= dlt), val, 0.0)

    o = o + _dot(a, v)
    b_last = bc[c - 1:c, :]
    k_tail = k * jnp.exp(b_last - bc)
    st_new = st * jnp.exp(b_last) + _dot(v, k_tail, _TN)
    st_ref[...] = st_new
    o = o * lax.rsqrt(jnp.mean(o * o, axis=-1, keepdims=True) + NORM_EPS) * ng_ref[...] * _silu(r_ref[...])
    o_ref[...] = o.astype(o_ref.dtype)

    @pl.when(ci == pl.num_programs(2) - 1)
    def _():
        s_ref[...] = st_new.T


def gla_core(proj, glr, w_gate, b_gate, norm_g, s0, o_prev, *, row0, nb, t, c, n_heads):
    has_state = s0 is not None
    m = proj.shape[0]
    dk = w_gate.shape[1] // n_heads
    dv = 2 * dk
    nc = t // c
    r0 = row0 // c
    rows = lambda b, h, i: r0 + b * nc + i
    in_specs = [
        pl.BlockSpec((c, dk), lambda b, h, i: (rows(b, h, i), h)),
        pl.BlockSpec((c, dk), lambda b, h, i: (rows(b, h, i), n_heads + h)),
        pl.BlockSpec((c, dv), lambda b, h, i: (rows(b, h, i), n_heads + h)),
        pl.BlockSpec((c, dv), lambda b, h, i: (rows(b, h, i), 2 * n_heads + h)),
        pl.BlockSpec((c, LANE), lambda b, h, i: (rows(b, h, i), 0)),
        pl.BlockSpec((LANE, dk), lambda b, h, i: (0, h)),
        pl.BlockSpec((1, dk), lambda b, h, i: (0, h)),
        pl.BlockSpec((1, dv), lambda b, h, i: (0, 0)),
    ]
    args = [proj, proj, proj, proj, glr, w_gate, b_gate.reshape(1, -1), norm_g.reshape(1, dv)]
    aliases = {}
    if has_state:
        in_specs += [
            pl.BlockSpec((None, None, dk, dv), lambda b, h, i: (b, h, 0, 0)),
            pl.BlockSpec(memory_space=pl.ANY),
        ]
        args += [s0, o_prev]
        aliases = {9: 0}
    return pl.pallas_call(
        functools.partial(_gla_kernel, c=c, has_state=has_state),
        grid=(nb, n_heads, nc),
        in_specs=in_specs,
        out_specs=[
            pl.BlockSpec((c, dv), lambda b, h, i: (rows(b, h, i), h)),
            pl.BlockSpec((None, None, dk, dv), lambda b, h, i: (b, h, 0, 0)),
        ],
        out_shape=[
            jax.ShapeDtypeStruct((m, n_heads * dv), BF16),
            jax.ShapeDtypeStruct((nb, n_heads, dk, dv), F32),
        ],
        scratch_shapes=[pltpu.VMEM((dv, dk), F32)],
        input_output_aliases=aliases,
        compiler_params=_params("parallel", "parallel", "arbitrary"),
        name="gla_core",
    )(*args)


GDN_CHUNK = 128
GLA_CHUNK = 64
FOX_BLOCK = 512
MM_TM, MM_TN, MM_TK = 1536, 1024, 1024


def _split_w_in(w_in, n_main):
    tail = w_in[:, n_main:]
    tail = jnp.pad(tail, ((0, 0), (0, LANE - tail.shape[1])))
    return w_in[:, :n_main].astype(BF16), tail.astype(BF16)


def _lanes(v, start):
    return jnp.pad(v, (start, LANE - start - v.shape[0])).reshape(1, LANE)


def _gdn_layer(x, g, w_in, conv_w, a_log, dt_bias, norm_g, w_out, conv_buf, s0, dims):
    bp, tp, bs, ts = dims
    mp, m = bp * tp, x.shape[0]
    nv = a_log.shape[0]
    conv_dim = conv_w.shape[1]
    v_dim = nv * GDN_HEAD_DIM
    w_main, w_tail = _split_w_in(w_in, conv_dim + v_dim)
    proj, tail = rms_mm(x, g, w_main, w_tail)
    ng = nv // GDN_GROUP_V
    gates = jnp.concatenate([
        tail[:, :nv].reshape(m, ng, GDN_GROUP_V), tail[:, nv:2 * nv].reshape(m, ng, GDN_GROUP_V),
        jnp.zeros((m, ng, LANE - 2 * GDN_GROUP_V), F32)], axis=-1).transpose(1, 0, 2)
    pad = ((0, 0), (GDN_GROUP_V, LANE - 2 * GDN_GROUP_V))
    gate_par = jnp.stack([jnp.pad(a_log.reshape(ng, GDN_GROUP_V), pad),
                          jnp.pad(dt_bias.reshape(ng, GDN_GROUP_V), pad)], axis=1)
    o, s_p = gdn_core(proj, gates, conv_w, gate_par, norm_g, None, None, None,
                      row0=0, nb=bp, t=tp, c=min(GDN_CHUNK, tp), n_v_heads=nv)
    o, s_s = gdn_core(proj, gates, conv_w, gate_par, norm_g, conv_buf, s0, o,
                      row0=mp, nb=bs, t=ts, c=ts, n_v_heads=nv)
    assert tp >= CONV_WIDTH - 1 and ts >= CONV_WIDTH - 1
    conv_p = proj[:mp, :conv_dim].reshape(bp, tp, conv_dim)[:, tp - (CONV_WIDTH - 1):]
    conv_s = proj[mp:, :conv_dim].reshape(bs, ts, conv_dim)[:, ts - (CONV_WIDTH - 1):]
    x = mm_res(o, w_out.astype(BF16), x, tm=MM_TM, tn=MM_TN, tk=MM_TK)
    return x, (conv_p, s_p), (conv_s, s_s)


def _fox_layer(x, g, w_in, b_f, w_out, k_cache, v_cache, lyr, lf_past, dims):
    bp, tp, bs, ts = dims
    mp = bp * tp
    nh = b_f.shape[0]
    hd = FOX_HEAD_DIM
    fd = nh * hd
    npast = k_cache.shape[2]
    w_main, w_tail = _split_w_in(w_in, 3 * fd)
    proj, tail = rms_mm(x, g, w_main, w_tail)
    bias = _lanes(b_f, 0)
    lf_p, c_p = fox_gate(tail, bias, None, row0=0, nb=bp, t=tp)
    c_past = fox_gate(lf_past.reshape(bs * npast, nh), None, None, row0=0, nb=bs, t=npast)
    c_past = c_past.reshape(bs, npast, nh)
    init = jnp.pad(c_past[:, npast - 1:, :], ((0, 0), (0, 0), (0, LANE - nh)))
    lf_s, c_s = fox_gate(tail, bias, init, row0=mp, nb=bs, t=ts)

    def layouts(c, b, t):
        ct = c[:, :nh].reshape(b, t, nh).transpose(0, 2, 1)
        return jnp.broadcast_to(ct[..., None], (b, nh, t, LANE)), ct[:, :, None, :]

    cq_p, ck_p = layouts(c_p, bp, tp)
    cq_s, ck_s = layouts(c_s, bs, ts)
    ck_past = c_past.transpose(0, 2, 1)[:, :, None, :]
    o = fox_flash(proj, cq_p, ck_p, nb=bp, t=tp, n_heads=nh, blk=min(FOX_BLOCK, tp))
    flat = lambda cache: cache[lyr].astype(BF16).reshape(bs, npast, fd)
    o, k_s, v_s = fox_sample(proj, flat(k_cache), flat(v_cache), cq_s, ck_past, ck_s, o,
                             row0=mp, nb=bs, t=ts, n_heads=nh)
    x = mm_res(o, w_out.astype(BF16), x, tm=MM_TM, tn=MM_TN, tk=MM_TK)
    res_p = (proj[:mp, fd:2 * fd].reshape(bp, tp, nh, hd), proj[:mp, 2 * fd:3 * fd].reshape(bp, tp, nh, hd),
             lf_p[:, :nh].reshape(bp, tp, nh))
    res_s = (k_s.reshape(bs, ts, nh, hd), v_s.reshape(bs, ts, nh, hd), lf_s[:, :nh].reshape(bs, ts, nh))
    return x, res_p, res_s


def _gla_layer(x, g, w_in, w_gate, b_gate, norm_g, w_out, s0, dims):
    bp, tp, bs, ts = dims
    mp = bp * tp
    nh = s0.shape[1]
    kd = w_gate.shape[1]
    w_main, w_tail = _split_w_in(w_in, 6 * kd)
    proj, glr = rms_mm(x, g, w_main, w_tail)
    wg = jnp.pad(w_gate, ((0, LANE - w_gate.shape[0]), (0, 0))).astype(BF16)
    o, s_p = gla_core(proj, glr, wg, b_gate, norm_g, None, None,
                      row0=0, nb=bp, t=tp, c=min(GLA_CHUNK, tp), n_heads=nh)
    o, s_s = gla_core(proj, glr, wg, b_gate, norm_g, s0, o, row0=mp, nb=bs, t=ts, c=ts, n_heads=nh)
    x = mm_res(o, w_out.astype(BF16), x, tm=MM_TM, tn=MM_TN, tk=MM_TK)
    return x, s_p, s_s


def kernel(x_prompt, x_sample, mem_prompt, state_gdn_conv, state_gdn_s, cache_fox_k, cache_fox_v, cache_fox_logf, state_gla_s, cache_mem_k, cache_mem_v, norm_mix, norm_mem, norm_memsrc, norm_ffn, norm_final, gdn_w_in, gdn_conv_w, gdn_a_log, gdn_dt_bias, gdn_norm, gdn_w_out, fox_w_in, fox_b_f, fox_w_out, gla_w_in, gla_w_gate, gla_b_gate, gla_norm, gla_w_out, mem_w_q, mem_w_kv, mem_w_o, ffn_w_up, ffn_w_down):
    bp, tp, d = x_prompt.shape
    bs, ts, _ = x_sample.shape
    dims = (bp, tp, bs, ts)
    mp, ms = bp * tp, bs * ts
    depth = norm_mix.shape[0]
    mem_len = mem_prompt.shape[1]
    mem_dim = mem_w_q.shape[2]
    x = jnp.concatenate([x_prompt.reshape(mp, d), x_sample.reshape(ms, d)], axis=0)
    mem = mem_prompt.reshape(bp * mem_len, d)

    gdn_p, gdn_s, fox_p, fox_s, gla_p, gla_s, kv_p = [], [], [], [], [], [], []
    for i in range(depth):
        j, kind = divmod(i, 3)
        if kind == 0:
            x, res_p, res_s = _gdn_layer(x, norm_mix[i], gdn_w_in[j], gdn_conv_w[j], gdn_a_log[j], gdn_dt_bias[j],
                                         gdn_norm[j], gdn_w_out[j], state_gdn_conv[j], state_gdn_s[j], dims)
            gdn_p.append(res_p)
            gdn_s.append(res_s)
        elif kind == 1:
            x, res_p, res_s = _fox_layer(x, norm_mix[i], fox_w_in[j], fox_b_f[j], fox_w_out[j],
                                         cache_fox_k, cache_fox_v, j, cache_fox_logf[j], dims)
            fox_p.append(res_p)
            fox_s.append(res_s)
        else:
            x, res_p, res_s = _gla_layer(x, norm_mix[i], gla_w_in[j], gla_w_gate[j], gla_b_gate[j], gla_norm[j],
                                         gla_w_out[j], state_gla_s[j], dims)
            gla_p.append(res_p)
            gla_s.append(res_s)

        kv = rms_mm(mem, norm_memsrc[i], mem_w_kv[i].astype(BF16)).reshape(bp, mem_len, 2 * mem_dim)
        kv_p.append(kv)
        wq = mem_w_q[i].astype(BF16)
        wo = mem_w_o[i].astype(BF16)
        x = mem_attn(x, norm_mem[i], wq, kv, kv, wo, row0=0, nb=bp, t=tp, tq=min(256, tp),
                     kmap=lambda b, q: (b, 0, 0), vmap=lambda b, q: (b, 0, 1), kblk=mem_dim)
        x = mem_attn(x, norm_mem[i], wq, cache_mem_k[i].reshape(bs, mem_len, mem_dim),
                     cache_mem_v[i].reshape(bs, mem_len, mem_dim), wo, row0=mp, nb=bs, t=ts, tq=ts,
                     kmap=lambda b, q: (b, 0, 0), vmap=lambda b, q: (b, 0, 0), kblk=mem_dim)

        u = rms_mm(x, norm_ffn[i], ffn_w_up[i].astype(BF16), act="relu2", out_dtype=BF16)
        x = mm_res(u, ffn_w_down[i].astype(BF16), x, tm=MM_TM, tn=MM_TN, tk=MM_TK)

    y_p, y_s = rmsnorm_split(x, norm_final, mp)
    kv_all = jnp.stack(kv_p)
    mem_shape = (depth, bp, mem_len, MEM_HEADS, MEM_HEAD_DIM)
    stack = lambda items, k: jnp.stack([it[k] for it in items])
    return (
        y_p.reshape(bp, tp, d), y_s.reshape(bs, ts, d),
        stack(gdn_p, 0), stack(gdn_p, 1), stack(fox_p, 0), stack(fox_p, 1), stack(fox_p, 2), jnp.stack(gla_p),
        kv_all[..., :mem_dim].reshape(mem_shape), kv_all[..., mem_dim:].reshape(mem_shape),
        stack(gdn_s, 0), stack(gdn_s, 1), stack(fox_s, 0), stack(fox_s, 1), stack(fox_s, 2), jnp.stack(gla_s),
    )
```

```python
import functools
import math

import jax
import jax.numpy as jnp
from jax import lax
from jax.experimental import pallas as pl
from jax.experimental.pallas import tpu as pltpu

F32 = jnp.float32
BF16 = jnp.bfloat16
NORM_EPS = 1e-6
LANE = 128
CONV_WIDTH = 4
GDN_HEAD_DIM = 128
FOX_HEAD_DIM = 128
FOX_HEADS = 32
GLA_HEADS = 4
GLA_TAU = 16.0
GLA_SUB = 16
MEM_HEADS = 4
MEM_HEAD_DIM = 128
VMEM_LIMIT = 56 * 1024 * 1024

_NT = (((1,), (1,)), ((), ()))
_TN = (((0,), (0,)), ((), ()))


def _params(*sem):
    return pltpu.CompilerParams(dimension_semantics=sem, vmem_limit_bytes=VMEM_LIMIT)


def _pick(n, pref):
    t = pref
    while n % t:
        assert t % 2 == 0 and t > 8, (n, pref)
        t //= 2
    return t


def _dot(a, b, dims=None):
    a = a.astype(BF16)
    b = b.astype(BF16)
    if dims is None:
        return jnp.dot(a, b, preferred_element_type=F32)
    return lax.dot_general(a, b, dims, preferred_element_type=F32)


def _split3(x):
    hi = x.astype(BF16)
    r = x - hi.astype(F32)
    mid = r.astype(BF16)
    lo = (r - mid.astype(F32)).astype(BF16)
    return hi, mid, lo


def _cumsum_rows(x):
    n = x.shape[0]
    tri = (lax.broadcasted_iota(jnp.int32, (n, n), 1) <= lax.broadcasted_iota(jnp.int32, (n, n), 0))
    tri = jnp.where(tri, 1.0, 0.0).astype(BF16)
    hi, mid, lo = _split3(x)
    out = jnp.dot(tri, lo, preferred_element_type=F32)
    out = out + jnp.dot(tri, mid, preferred_element_type=F32)
    return out + jnp.dot(tri, hi, preferred_element_type=F32)


def _log_sigmoid(x):
    return jnp.minimum(x, 0.0) - jnp.log1p(jnp.exp(-jnp.abs(x)))


def _softplus(x):
    return jnp.maximum(x, 0.0) + jnp.log1p(jnp.exp(-jnp.abs(x)))


def _sigmoid(x):
    return 1.0 / (1.0 + jnp.exp(-x))


def _silu(x):
    return x * _sigmoid(x)


def _rms_rows(x, g):
    return x * lax.rsqrt(jnp.mean(x * x, axis=-1, keepdims=True) + NORM_EPS) * g


def _rms_mm_kernel(*refs, act, has_tail):
    if has_tail:
        x_ref, g_ref, w_ref, wt_ref, o_ref, ot_ref, h_ref = refs
    else:
        x_ref, g_ref, w_ref, o_ref, h_ref = refs

    @pl.when(pl.program_id(1) == 0)
    def _():
        h_ref[...] = _rms_rows(x_ref[...], g_ref[...]).astype(BF16)
        if has_tail:
            ot_ref[...] = jnp.dot(h_ref[...], wt_ref[...], preferred_element_type=F32)

    acc = jnp.dot(h_ref[...], w_ref[...], preferred_element_type=F32)
    if act == "relu2":
        acc = jnp.square(jnp.maximum(acc, 0.0))
    o_ref[...] = acc.astype(o_ref.dtype)


def rms_mm(x, g, w, w_tail=None, *, lyr=None, act=None, out_dtype=F32, tm=512, tn=1024):
    m, k = x.shape
    n = w.shape[-1]
    tm, tn = _pick(m, tm), _pick(n, tn)
    has_tail = w_tail is not None
    if lyr is None:
        w_spec = pl.BlockSpec((k, tn), lambda i, j: (0, j))
    else:
        w_spec = pl.BlockSpec((None, k, tn), lambda i, j: (lyr, 0, j))
    in_specs = [
        pl.BlockSpec((tm, k), lambda i, j: (i, 0)),
        pl.BlockSpec((1, k), lambda i, j: (0, 0)),
        w_spec,
    ]
    out_specs = [pl.BlockSpec((tm, tn), lambda i, j: (i, j))]
    out_shape = [jax.ShapeDtypeStruct((m, n), out_dtype)]
    args = [x, g.reshape(1, k), w]
    if has_tail:
        in_specs.append(pl.BlockSpec((k, LANE), lambda i, j: (0, 0)))
        out_specs.append(pl.BlockSpec((tm, LANE), lambda i, j: (i, 0)))
        out_shape.append(jax.ShapeDtypeStruct((m, LANE), F32))
        args.append(w_tail)
    res = pl.pallas_call(
        functools.partial(_rms_mm_kernel, act=act, has_tail=has_tail),
        grid=(m // tm, n // tn),
        in_specs=in_specs,
        out_specs=out_specs,
        out_shape=out_shape,
        scratch_shapes=[pltpu.VMEM((tm, k), BF16)],
        compiler_params=_params("parallel", "arbitrary"),
        name="rms_mm",
    )(*args)
    return res if has_tail else res[0]


def _mm_res_kernel(a_ref, w_ref, r_ref, o_ref):
    @pl.when(pl.program_id(2) == 0)
    def _():
        o_ref[...] = r_ref[...]

    o_ref[...] += jnp.dot(a_ref[...], w_ref[...], preferred_element_type=F32)


def mm_res(a, w, res, *, lyr=None, tm=512, tn=512, tk=None):
    m, k = a.shape
    n = w.shape[-1]
    tm, tn, tk = _pick(m, tm), _pick(n, tn), _pick(k, k if tk is None else tk)
    if lyr is None:
        w_spec = pl.BlockSpec((tk, tn), lambda i, j, l: (l, j))
    else:
        w_spec = pl.BlockSpec((None, tk, tn), lambda i, j, l: (lyr, l, j))
    return pl.pallas_call(
        _mm_res_kernel,
        grid=(m // tm, n // tn, k // tk),
        in_specs=[
            pl.BlockSpec((tm, tk), lambda i, j, l: (i, l)),
            w_spec,
            pl.BlockSpec((tm, tn), lambda i, j, l: (i, j)),
        ],
        out_specs=pl.BlockSpec((tm, tn), lambda i, j, l: (i, j)),
        out_shape=jax.ShapeDtypeStruct((m, n), F32),
        input_output_aliases={2: 0},
        compiler_params=_params("parallel", "parallel", "arbitrary"),
        name="mm_res",
    )(a, w, res)


def _rmsnorm_split_kernel(x_ref, g_ref, a_ref, b_ref, *, na):
    y = _rms_rows(x_ref[...], g_ref[...])

    @pl.when(pl.program_id(0) < na)
    def _():
        a_ref[...] = y

    @pl.when(pl.program_id(0) >= na)
    def _():
        b_ref[...] = y


def rmsnorm_split(x, g, ma, *, tm=512):
    m, k = x.shape
    mb = m - ma
    tm = _pick(math.gcd(ma, mb), tm)
    na = ma // tm
    return pl.pallas_call(
        functools.partial(_rmsnorm_split_kernel, na=na),
        grid=(m // tm,),
        in_specs=[pl.BlockSpec((tm, k), lambda i: (i, 0)), pl.BlockSpec((1, k), lambda i: (0, 0))],
        out_specs=[pl.BlockSpec((tm, k), lambda i: (jnp.minimum(i, na - 1), 0)),
                   pl.BlockSpec((tm, k), lambda i: (jnp.maximum(i - na, 0), 0))],
        out_shape=[jax.ShapeDtypeStruct((ma, k), F32), jax.ShapeDtypeStruct((mb, k), F32)],
        compiler_params=_params("arbitrary"),
        name="rmsnorm_split",
    )(x, g.reshape(1, k))


def _mem_attn_kernel(x_ref, g_ref, wq_ref, k_ref, v_ref, wo_ref, o_ref):
    x = x_ref[...]
    h = _rms_rows(x, g_ref[...]).astype(BF16)
    q = jnp.dot(h, wq_ref[...], preferred_element_type=F32)
    k = k_ref[...].astype(BF16)
    v = v_ref[...].astype(BF16)
    outs = []
    for hh in range(MEM_HEADS):
        sl = slice(hh * MEM_HEAD_DIM, (hh + 1) * MEM_HEAD_DIM)
        s = _dot(q[:, sl], k[:, sl], _NT) * MEM_HEAD_DIM ** -0.5
        p = jnp.exp(s - jnp.max(s, axis=-1, keepdims=True))
        p = p / jnp.sum(p, axis=-1, keepdims=True)
        outs.append(_dot(p, v[:, sl]))
    o = jnp.concatenate(outs, axis=1).astype(BF16)
    o_ref[...] = x + jnp.dot(o, wo_ref[...], preferred_element_type=F32)


def mem_attn(x, g, wq, k, v, wo, *, row0, nb, t, tq, kmap, vmap, kblk):
    m, d = x.shape
    nq = t // tq
    r0 = row0 // tq
    dm = wq.shape[1]
    mem_len = k.shape[1]
    xmap = lambda b, i: (r0 + b * nq + i, 0)
    const = lambda b, i: (0, 0)
    return pl.pallas_call(
        _mem_attn_kernel,
        grid=(nb, nq),
        in_specs=[
            pl.BlockSpec((tq, d), xmap),
            pl.BlockSpec((1, d), const),
            pl.BlockSpec((d, dm), const),
            pl.BlockSpec((None, mem_len, kblk), kmap),
            pl.BlockSpec((None, mem_len, kblk), vmap),
            pl.BlockSpec((dm, d), const),
        ],
        out_specs=pl.BlockSpec((tq, d), xmap),
        out_shape=jax.ShapeDtypeStruct((m, d), F32),
        input_output_aliases={0: 0},
        compiler_params=_params("parallel", "parallel"),
        name="mem_attn",
    )(x, g.reshape(1, d), wq, k, v, wo)


def _fox_gate_kernel(*refs, gate, has_init):
    refs = list(refs)
    x_ref = refs.pop(0)
    bias_ref = refs.pop(0) if gate else None
    init_ref = refs.pop(0) if has_init else None
    lf_ref = refs.pop(0) if gate else None
    c_ref, carry_ref = refs

    @pl.when(pl.program_id(1) == 0)
    def _():
        carry_ref[...] = init_ref[...] if has_init else jnp.zeros_like(carry_ref)

    x = x_ref[...]
    if gate:
        x = _log_sigmoid(x + bias_ref[...])
        lf_ref[...] = x
    c = _cumsum_rows(x) + carry_ref[...]
    c_ref[...] = c
    carry_ref[...] = c[-1:, :]


def fox_gate(x, bias, init, *, row0, nb, t):
    w = x.shape[1]
    tb = min(t, 256)
    nt = t // tb
    r0 = row0 // tb
    gate = bias is not None
    has_init = init is not None
    xmap = lambda b, i: (r0 + b * nt + i, 0)
    omap = lambda b, i: (b * nt + i, 0)
    in_specs = [pl.BlockSpec((tb, w), xmap)]
    args = [x]
    if gate:
        in_specs.append(pl.BlockSpec((1, w), lambda b, i: (0, 0)))
        args.append(bias)
    if has_init:
        in_specs.append(pl.BlockSpec((None, 1, w), lambda b, i: (b, 0, 0)))
        args.append(init)
    n_out = 2 if gate else 1
    res = pl.pallas_call(
        functools.partial(_fox_gate_kernel, gate=gate, has_init=has_init),
        grid=(nb, nt),
        in_specs=in_specs,
        out_specs=[pl.BlockSpec((tb, w), omap)] * n_out,
        out_shape=[jax.ShapeDtypeStruct((nb * t, w), F32)] * n_out,
        scratch_shapes=[pltpu.VMEM((1, w), F32)],
        compiler_params=_params("parallel", "arbitrary"),
        name="fox_gate",
    )(*args)
    return res if gate else res[0]


def _fox_flash_kernel(q_ref, k_ref, v_ref, cq_ref, ck_ref, prev_ref, o_ref, *, blk):
    del prev_ref
    t = q_ref.shape[0]
    nq = t // blk
    q_b = q_ref[...].astype(BF16)
    k_b = k_ref[...].astype(BF16)
    v_b = v_ref[...].astype(BF16)
    cq = cq_ref[...]
    ck = ck_ref[...]
    row = lax.broadcasted_iota(jnp.int32, (blk, blk), 0)
    col = lax.broadcasted_iota(jnp.int32, (blk, blk), 1)
    m, l, acc = [None] * nq, [None] * nq, [None] * nq
    for j in range(nq):
        ks = slice(j * blk, (j + 1) * blk)
        for i in range(j, nq):
            qs = slice(i * blk, (i + 1) * blk)
            s = lax.dot_general(q_b[qs], k_b[ks], _NT, preferred_element_type=F32) * FOX_HEAD_DIM ** -0.5
            s = s + (jnp.concatenate([cq[qs]] * (blk // LANE), axis=1) - ck[:, ks])
            if i == j:
                s = jnp.where(col <= row, s, -jnp.inf)
            s_max = jnp.max(s, axis=-1, keepdims=True)
            if j == 0:
                m[i] = s_max
                p = jnp.exp(s - s_max)
                l[i] = jnp.sum(p, axis=-1, keepdims=True)
                acc[i] = jnp.dot(p.astype(BF16), v_b[ks], preferred_element_type=F32)
            else:
                m_new = jnp.maximum(m[i], s_max)
                alpha = jnp.exp(m[i] - m_new)
                p = jnp.exp(s - m_new)
                l[i] = alpha * l[i] + jnp.sum(p, axis=-1, keepdims=True)
                acc[i] = alpha * acc[i] + jnp.dot(p.astype(BF16), v_b[ks], preferred_element_type=F32)
                m[i] = m_new
    o_ref[...] = jnp.concatenate([acc[i] / l[i] for i in range(nq)], axis=0).astype(o_ref.dtype)


def fox_flash(proj, cq_b, ck_row, o_prev, *, nb, t, n_heads, blk=512):
    assert t % blk == 0 and blk % LANE == 0
    d = FOX_HEAD_DIM
    return pl.pallas_call(
        functools.partial(_fox_flash_kernel, blk=blk),
        grid=(nb, n_heads),
        in_specs=[
            pl.BlockSpec((t, d), lambda b, h: (b, h)),
            pl.BlockSpec((t, d), lambda b, h: (b, n_heads + h)),
            pl.BlockSpec((t, d), lambda b, h: (b, 2 * n_heads + h)),
            pl.BlockSpec((None, None, t, LANE), lambda b, h: (b, h, 0, 0)),
            pl.BlockSpec((None, None, 1, t), lambda b, h: (b, h, 0, 0)),
            pl.BlockSpec(memory_space=pl.ANY),
        ],
        out_specs=pl.BlockSpec((t, d), lambda b, h: (b, h)),
        out_shape=jax.ShapeDtypeStruct(o_prev.shape, o_prev.dtype),
        input_output_aliases={5: 0},
        compiler_params=_params("parallel", "parallel"),
        name="fox_flash",
    )(proj, proj, proj, cq_b, ck_row, o_prev)


FOX_SAMPLE_GROUP = 8


def _fox_sample_kernel(q_ref, kn_ref, vn_ref, kc_hbm, vc_hbm, cq_ref, ckp_ref, ckn_ref, prev_ref,
                       o_ref, ko_ref, vo_ref, kbuf, vbuf, sem, *, lyr):
    del prev_ref
    t = q_ref.shape[0]
    d = FOX_HEAD_DIM
    hg = FOX_SAMPLE_GROUP
    scale = d ** -0.5
    npast = kbuf.shape[2]
    ng = pl.num_programs(1)
    step = pl.program_id(0) * ng + pl.program_id(1)
    slot = lax.rem(step, 2)

    def past_copies(s, into):
        bb = lax.div(s, ng)
        h0 = lax.rem(s, ng) * hg
        cps = []
        for hh in range(hg):
            cps.append(pltpu.make_async_copy(kc_hbm.at[lyr, bb, :, h0 + hh, :], kbuf.at[into, hh], sem.at[into, 0]))
            cps.append(pltpu.make_async_copy(vc_hbm.at[lyr, bb, :, h0 + hh, :], vbuf.at[into, hh], sem.at[into, 1]))
        return cps

    @pl.when(step == 0)
    def _():
        for cp in past_copies(step, slot):
            cp.start()

    @pl.when(step + 1 < pl.num_programs(0) * ng)
    def _():
        for cp in past_copies(step + 1, 1 - slot):
            cp.start()

    for cp in past_copies(step, slot):
        cp.wait()

    row = lax.broadcasted_iota(jnp.int32, (t, t), 0)
    col = lax.broadcasted_iota(jnp.int32, (t, t), 1)
    outs = []
    for hh in range(hg):
        sl = slice(hh * d, (hh + 1) * d)
        q = q_ref[:, sl]
        kn = kn_ref[:, sl]
        vn = vn_ref[:, sl]
        ko_ref[:, hh, :] = kn
        vo_ref[:, hh, :] = vn
        cq = cq_ref[hh]
        cq_wide = jnp.concatenate([cq] * (npast // LANE), axis=1)
        s_p = _dot(q, kbuf[slot, hh], _NT) * scale + (cq_wide - ckp_ref[hh])
        s_n = _dot(q, kn, _NT) * scale + (cq[:, :t] - ckn_ref[hh])
        s_n = jnp.where(col <= row, s_n, -jnp.inf)
        m = jnp.maximum(jnp.max(s_p, axis=-1, keepdims=True), jnp.max(s_n, axis=-1, keepdims=True))
        p_p = jnp.exp(s_p - m)
        p_n = jnp.exp(s_n - m)
        inv = 1.0 / (jnp.sum(p_p, axis=-1, keepdims=True) + jnp.sum(p_n, axis=-1, keepdims=True))
        outs.append(_dot(p_p * inv, vbuf[slot, hh]) + _dot(p_n * inv, vn))
    o_ref[...] = jnp.concatenate(outs, axis=1).astype(o_ref.dtype)


def fox_sample(proj, k_cache, v_cache, lyr, cq_b, ck_past, ck_new, o_prev, *, row0, nb, t, n_heads):
    d = FOX_HEAD_DIM
    hg = FOX_SAMPLE_GROUP
    ng = n_heads // hg
    r0 = row0 // t
    npast = k_cache.shape[2]
    bias = lambda b, g: (b, g, 0, 0)
    return pl.pallas_call(
        functools.partial(_fox_sample_kernel, lyr=lyr),
        grid=(nb, ng),
        in_specs=[
            pl.BlockSpec((t, hg * d), lambda b, g: (r0 + b, g)),
            pl.BlockSpec((t, hg * d), lambda b, g: (r0 + b, ng + g)),
            pl.BlockSpec((t, hg * d), lambda b, g: (r0 + b, 2 * ng + g)),
            pl.BlockSpec(memory_space=pl.ANY),
            pl.BlockSpec(memory_space=pl.ANY),
            pl.BlockSpec((None, hg, t, LANE), bias),
            pl.BlockSpec((None, hg, 1, npast), bias),
            pl.BlockSpec((None, hg, 1, t), bias),
            pl.BlockSpec(memory_space=pl.ANY),
        ],
        out_specs=[pl.BlockSpec((t, hg * d), lambda b, g: (r0 + b, g)),
                   pl.BlockSpec((t, hg, d), lambda b, g: (b, g, 0)),
                   pl.BlockSpec((t, hg, d), lambda b, g: (b, g, 0))],
        out_shape=[
            jax.ShapeDtypeStruct(o_prev.shape, o_prev.dtype),
            jax.ShapeDtypeStruct((nb * t, n_heads, d), F32),
            jax.ShapeDtypeStruct((nb * t, n_heads, d), F32),
        ],
        scratch_shapes=[pltpu.VMEM((2, hg, npast, d), F32), pltpu.VMEM((2, hg, npast, d), F32),
                        pltpu.SemaphoreType.DMA((2, 2))],
        input_output_aliases={8: 0},
        compiler_params=_params("arbitrary", "arbitrary"),
        name="fox_sample",
    )(proj, proj, proj, k_cache, v_cache, cq_b, ck_past, ck_new, o_prev)


GDN_GROUP_V = 8
GDN_INV_BASE = 16


def _transpose_lanes(x):
    eye = (lax.broadcasted_iota(jnp.int32, (LANE, LANE), 0) == lax.broadcasted_iota(jnp.int32, (LANE, LANE), 1))
    eye = jnp.where(eye, 1.0, 0.0).astype(BF16)
    hi, mid, lo = _split3(x)
    out = lax.dot_general(eye, lo, _NT, preferred_element_type=F32)
    out = out + lax.dot_general(eye, mid, _NT, preferred_element_type=F32)
    return out + lax.dot_general(eye, hi, _NT, preferred_element_type=F32)


def _unit_lower_inverses(mats):
    n = mats[0].shape[0]
    row = lax.broadcasted_iota(jnp.int32, (n, n), 0)
    col = lax.broadcasted_iota(jnp.int32, (n, n), 1)
    blk = min(GDN_INV_BASE, n)
    shift = int(math.log2(blk))
    same = (row >> shift) == (col >> shift)
    eye = jnp.where(row == col, 1.0, 0.0).astype(F32)
    ps = [jnp.where(same, a, 0.0).astype(BF16) for a in mats]
    ts = [eye - p.astype(F32) for p in ps]
    for _ in range(shift - 1):
        ps = [jnp.dot(p, p, preferred_element_type=F32).astype(BF16) for p in ps]
        ts = [t + jnp.dot(t.astype(BF16), p, preferred_element_type=F32) for t, p in zip(ts, ps)]
    while blk < n:
        blk *= 2
        shift += 1
        same2 = (row >> shift) == (col >> shift)
        off = jnp.logical_and(same2, jnp.logical_not(same))
        ms = [jnp.where(off, a, 0.0).astype(BF16) for a in mats]
        tbs = [t.astype(BF16) for t in ts]
        tms = [jnp.dot(tb, m, preferred_element_type=F32).astype(BF16) for tb, m in zip(tbs, ms)]
        ts = [t - jnp.dot(tm, tb, preferred_element_type=F32) for t, tm, tb in zip(ts, tms, tbs)]
        same = same2
    return ts


def _gdn_kernel(*refs, c, has_state):
    refs = list(refs)
    q_ref, k_ref, v_ref, z_ref, gate_ref, wq_ref, wk_ref, wv_ref, gp_ref, ng_ref = refs[:10]
    refs = refs[10:]
    if has_state:
        bq_ref, bk_ref, bv_ref, s0_ref = refs[:4]
        refs = refs[4:]
    _prev, o_ref, s_ref, cq_ref, ck_ref, cv_ref, xq_ref, xk_ref, xv_ref = refs
    d = GDN_HEAD_DIM

    @pl.when(pl.program_id(2) == 0)
    def _():
        for x_ref, b_ref in ((xq_ref, bq_ref if has_state else None), (xk_ref, bk_ref if has_state else None),
                             (xv_ref, bv_ref if has_state else None)):
            x_ref[0:8, :] = jnp.zeros((8, x_ref.shape[1]), F32)
            if has_state:
                x_ref[8 - (CONV_WIDTH - 1):8, :] = b_ref[...]
        s_ref[...] = s0_ref[...] if has_state else jnp.zeros_like(s_ref)

    def conv(x_ref, raw_ref, w_ref):
        x_ref[8:8 + c, :] = raw_ref[...]
        w = w_ref[...]
        out = x_ref[8:8 + c, :] * w[3:4, :]
        for i in range(CONV_WIDTH - 1):
            out = out + x_ref[5 + i:5 + i + c, :] * w[i:i + 1, :]
        x_ref[0:8, :] = x_ref[c:c + 8, :]
        return _silu(out)

    qc = conv(xq_ref, q_ref, wq_ref)
    kc = conv(xk_ref, k_ref, wk_ref)
    vc = conv(xv_ref, v_ref, wv_ref)

    @pl.when(pl.program_id(2) == pl.num_programs(2) - 1)
    def _():
        for c_ref, x_ref in ((cq_ref, xq_ref), (ck_ref, xk_ref), (cv_ref, xv_ref)):
            c_ref[...] = x_ref[8 - (CONV_WIDTH - 1):8, :]

    gt = gate_ref[...]
    gp = gp_ref[...]
    beta_all = _sigmoid(gt)
    g_all = -jnp.exp(gp[0:1, :]) * _softplus(gt + gp[1:2, :])
    gc_all = _cumsum_rows(g_all)
    gc_t = _transpose_lanes(gc_all)

    row = lax.broadcasted_iota(jnp.int32, (c, c), 0)
    col = lax.broadcasted_iota(jnp.int32, (c, c), 1)
    incl = col <= row
    strict = col < row
    ng = ng_ref[...]
    z = z_ref[...]

    heads = range(GDN_GROUP_V)
    q_hs, k_hs, kks, qks = [], [], [], []
    for hq in range(GDN_GROUP_V // 2):
        sl = slice(hq * d, (hq + 1) * d)
        q_h = qc[:, sl]
        k_h = kc[:, sl]
        q_h = q_h * lax.rsqrt(jnp.sum(q_h * q_h, axis=-1, keepdims=True) + NORM_EPS) * d ** -0.5
        k_h = k_h * lax.rsqrt(jnp.sum(k_h * k_h, axis=-1, keepdims=True) + NORM_EPS)
        k_b = k_h.astype(BF16)
        q_hs.append(q_h)
        k_hs.append(k_h)
        kks.append(lax.dot_general(k_b, k_b, _NT, preferred_element_type=F32))
        qks.append(lax.dot_general(q_h.astype(BF16), k_b, _NT, preferred_element_type=F32))

    betas = [beta_all[:, j:j + 1] for j in heads]
    gcols = [gc_all[:, 8 + j:9 + j] for j in heads]
    glasts = [gc_all[c - 1:c, 8 + j:9 + j] for j in heads]
    egs = [jnp.exp(gcols[j]) for j in heads]
    decays = [jnp.exp(jnp.where(incl, gcols[j] - gc_t[8 + j:9 + j, :], -jnp.inf)) for j in heads]
    tinvs = _unit_lower_inverses([jnp.where(strict, betas[j] * kks[j // 2] * decays[j], 0.0) for j in heads])
    rhss = [jnp.concatenate([k_hs[j // 2] * (betas[j] * egs[j]), vc[:, j * d:(j + 1) * d] * betas[j]], axis=1)
            for j in heads]
    sols = [_dot(tinvs[j], rhss[j]) for j in heads]
    s_olds = [s_ref[j] for j in heads]
    wss = [_dot(jnp.concatenate([sols[j][:, :d], q_hs[j // 2] * egs[j]], axis=0), s_olds[j]) for j in heads]
    u_news = [sols[j][:, d:] - wss[j][:c] for j in heads]
    os_ = [wss[j][c:] + _dot(qks[j // 2] * decays[j], u_news[j]) for j in heads]
    s_news = [s_olds[j] * jnp.exp(glasts[j])
              + _dot(k_hs[j // 2] * jnp.exp(glasts[j] - gcols[j]), u_news[j], _TN) for j in heads]
    s_ref[...] = jnp.stack(s_news, axis=0)
    outs = [o * lax.rsqrt(jnp.mean(o * o, axis=-1, keepdims=True) + NORM_EPS) * ng for o in os_]
    o_ref[...] = (jnp.concatenate(outs, axis=1) * _silu(z)).astype(o_ref.dtype)


def gdn_core(proj, gates, conv_w, gate_par, norm_g, conv_buf, s0, o_prev, *, row0, nb, t, c, n_v_heads):
    has_state = s0 is not None
    d = GDN_HEAD_DIM
    gv = GDN_GROUP_V
    ng = n_v_heads // gv
    wqk = (gv // 2) * d
    wv = gv * d
    nqk = (n_v_heads // 2) * d // wqk
    nc = t // c
    r0 = row0 // c
    m = proj.shape[0]
    rows = lambda b, g, i: r0 + b * nc + i
    in_specs = [
        pl.BlockSpec((c, wqk), lambda b, g, i: (rows(b, g, i), g)),
        pl.BlockSpec((c, wqk), lambda b, g, i: (rows(b, g, i), nqk + g)),
        pl.BlockSpec((c, wv), lambda b, g, i: (rows(b, g, i), ng + g)),
        pl.BlockSpec((c, wv), lambda b, g, i: (rows(b, g, i), 2 * ng + g)),
        pl.BlockSpec((None, c, LANE), lambda b, g, i: (g, rows(b, g, i), 0)),
        pl.BlockSpec((CONV_WIDTH, wqk), lambda b, g, i: (0, g)),
        pl.BlockSpec((CONV_WIDTH, wqk), lambda b, g, i: (0, nqk + g)),
        pl.BlockSpec((CONV_WIDTH, wv), lambda b, g, i: (0, ng + g)),
        pl.BlockSpec((None, 2, LANE), lambda b, g, i: (g, 0, 0)),
        pl.BlockSpec((1, d), lambda b, g, i: (0, 0)),
    ]
    args = [proj, proj, proj, proj, gates, conv_w, conv_w, conv_w, gate_par, norm_g.reshape(1, d)]
    if has_state:
        in_specs += [
            pl.BlockSpec((None, CONV_WIDTH - 1, wqk), lambda b, g, i: (b, 0, g)),
            pl.BlockSpec((None, CONV_WIDTH - 1, wqk), lambda b, g, i: (b, 0, nqk + g)),
            pl.BlockSpec((None, CONV_WIDTH - 1, wv), lambda b, g, i: (b, 0, ng + g)),
            pl.BlockSpec((None, gv, d, d), lambda b, g, i: (b, g, 0, 0)),
        ]
        args += [conv_buf, conv_buf, conv_buf, s0]
    in_specs.append(pl.BlockSpec(memory_space=pl.ANY))
    args.append(o_prev)
    aliases = {len(args) - 1: 0}
    return pl.pallas_call(
        functools.partial(_gdn_kernel, c=c, has_state=has_state),
        grid=(nb, ng, nc),
        in_specs=in_specs,
        out_specs=[
            pl.BlockSpec((c, wv), lambda b, g, i: (rows(b, g, i), g)),
            pl.BlockSpec((None, gv, d, d), lambda b, g, i: (b, g, 0, 0)),
            pl.BlockSpec((None, CONV_WIDTH - 1, wqk), lambda b, g, i: (b, 0, g)),
            pl.BlockSpec((None, CONV_WIDTH - 1, wqk), lambda b, g, i: (b, 0, g)),
            pl.BlockSpec((None, CONV_WIDTH - 1, wv), lambda b, g, i: (b, 0, g)),
        ],
        out_shape=[
            jax.ShapeDtypeStruct((m, n_v_heads * d), BF16),
            jax.ShapeDtypeStruct((nb, n_v_heads, d, d), F32),
            jax.ShapeDtypeStruct((nb, CONV_WIDTH - 1, nqk * wqk), F32),
            jax.ShapeDtypeStruct((nb, CONV_WIDTH - 1, nqk * wqk), F32),
            jax.ShapeDtypeStruct((nb, CONV_WIDTH - 1, ng * wv), F32),
        ],
        scratch_shapes=[pltpu.VMEM((c + 8, wqk), F32), pltpu.VMEM((c + 8, wqk), F32), pltpu.VMEM((c + 8, wv), F32)],
        input_output_aliases=aliases,
        compiler_params=_params("parallel", "parallel", "arbitrary"),
        name="gdn_core",
    )(*args)


def _gla_kernel(*refs, c, has_state):
    refs = list(refs)
    q_ref, k_ref, v_ref, r_ref, glr_ref, wg_ref, bg_ref, ng_ref = refs[:8]
    refs = refs[8:]
    if has_state:
        s0_ref = refs.pop(0)
    _prev, o_ref, s_ref, st_ref = refs
    dk = q_ref.shape[1]
    sub = min(GLA_SUB, c)
    ci = pl.program_id(2)

    @pl.when(ci == 0)
    def _():
        st_ref[...] = s0_ref[...].T if has_state else jnp.zeros_like(st_ref)

    glog = _log_sigmoid(_dot(glr_ref[...], wg_ref[...]) + bg_ref[...]) * (1.0 / GLA_TAU)
    bc = _cumsum_rows(glog)
    q = q_ref[...] * dk ** -0.5
    k = k_ref[...]
    v = v_ref[...].astype(BF16)
    st = st_ref[...]
    o = _dot(q * jnp.exp(bc), st, _NT)

    tok = lax.broadcasted_iota(jnp.int32, (c, 1), 0)
    a_rows = []
    for i in range(c // sub):
        lo = i * sub
        if i == 0:
            a_rows.append(jnp.zeros((sub, c), F32))
            continue
        ref_i = bc[lo:lo + 1, :]
        q_off = q[lo:lo + sub, :] * jnp.exp(bc[lo:lo + sub, :] - ref_i)
        k_off = k * jnp.exp(jnp.where(tok < lo, ref_i - bc, -jnp.inf))
        a_rows.append(_dot(q_off, k_off, _NT))
    a = jnp.concatenate(a_rows, axis=0)

    row = lax.broadcasted_iota(jnp.int32, (c, c), 0)
    col = lax.broadcasted_iota(jnp.int32, (c, c), 1)
    rmod = jnp.bitwise_and(row, sub - 1)
    for dlt in range(sub):
        k_r = k if dlt == 0 else pltpu.roll(k, dlt, 0)
        b_r = bc if dlt == 0 else pltpu.roll(bc, dlt, 0)
        e = jnp.exp(jnp.minimum(bc - b_r, 0.0))
        val = jnp.sum(q * k_r * e, axis=-1, keepdims=True)
        a = a + jnp.where(jnp.logical_and(col == row - dlt, rmod >= dlt), val, 0.0)

    o = o + _dot(a, v)
    b_last = bc[c - 1:c, :]
    k_tail = k * jnp.exp(b_last - bc)
    st_new = st * jnp.exp(b_last) + _dot(v, k_tail, _TN)
    st_ref[...] = st_new
    o = o * lax.rsqrt(jnp.mean(o * o, axis=-1, keepdims=True) + NORM_EPS) * ng_ref[...] * _silu(r_ref[...])
    o_ref[...] = o.astype(o_ref.dtype)

    @pl.when(ci == pl.num_programs(2) - 1)
    def _():
        s_ref[...] = st_new.T


def gla_core(proj, glr, w_gate, b_gate, norm_g, s0, o_prev, *, row0, nb, t, c, n_heads):
    has_state = s0 is not None
    m = proj.shape[0]
    dk = w_gate.shape[1] // n_heads
    dv = 2 * dk
    nc = t // c
    r0 = row0 // c
    rows = lambda b, h, i: r0 + b * nc + i
    in_specs = [
        pl.BlockSpec((c, dk), lambda b, h, i: (rows(b, h, i), h)),
        pl.BlockSpec((c, dk), lambda b, h, i: (rows(b, h, i), n_heads + h)),
        pl.BlockSpec((c, dv), lambda b, h, i: (rows(b, h, i), n_heads + h)),
        pl.BlockSpec((c, dv), lambda b, h, i: (rows(b, h, i), 2 * n_heads + h)),
        pl.BlockSpec((c, LANE), lambda b, h, i: (rows(b, h, i), 0)),
        pl.BlockSpec((LANE, dk), lambda b, h, i: (0, h)),
        pl.BlockSpec((1, dk), lambda b, h, i: (0, h)),
        pl.BlockSpec((1, dv), lambda b, h, i: (0, 0)),
    ]
    args = [proj, proj, proj, proj, glr, w_gate, b_gate.reshape(1, -1), norm_g.reshape(1, dv)]
    if has_state:
        in_specs.append(pl.BlockSpec((None, None, dk, dv), lambda b, h, i: (b, h, 0, 0)))
        args.append(s0)
    in_specs.append(pl.BlockSpec(memory_space=pl.ANY))
    args.append(o_prev)
    aliases = {len(args) - 1: 0}
    return pl.pallas_call(
        functools.partial(_gla_kernel, c=c, has_state=has_state),
        grid=(nb, n_heads, nc),
        in_specs=in_specs,
        out_specs=[
            pl.BlockSpec((c, dv), lambda b, h, i: (rows(b, h, i), h)),
            pl.BlockSpec((None, None, dk, dv), lambda b, h, i: (b, h, 0, 0)),
        ],
        out_shape=[
            jax.ShapeDtypeStruct((m, n_heads * dv), BF16),
            jax.ShapeDtypeStruct((nb, n_heads, dk, dv), F32),
        ],
        scratch_shapes=[pltpu.VMEM((dv, dk), F32)],
        input_output_aliases=aliases,
        compiler_params=_params("parallel", "parallel", "arbitrary"),
        name="gla_core",
    )(*args)


GDN_CHUNK = 128
GLA_CHUNK = 128
FOX_BLOCK = 512
MM_TM, MM_TN, MM_TK = 1536, 1024, 1024


def _split_w_in(w_in, n_main):
    tail = w_in[:, n_main:]
    tail = jnp.pad(tail, ((0, 0), (0, LANE - tail.shape[1])))
    return w_in[:, :n_main].astype(BF16), tail.astype(BF16)


def _out_proj(o, w_out, x):
    if isinstance(w_out, tuple):
        return mm_res(o, w_out[0], x, lyr=w_out[1], tm=MM_TM, tn=MM_TN, tk=MM_TK)
    return mm_res(o, w_out.astype(BF16), x, tm=MM_TM, tn=MM_TN, tk=MM_TK)


def _lanes(v, start):
    return jnp.pad(v, (start, LANE - start - v.shape[0])).reshape(1, LANE)


def _gdn_layer(x, g, w_in, conv_w, a_log, dt_bias, norm_g, w_out, conv_buf, s0, dims, o_buf):
    bp, tp, bs, ts = dims
    mp, m = bp * tp, x.shape[0]
    nv = a_log.shape[0]
    conv_dim = conv_w.shape[1]
    v_dim = nv * GDN_HEAD_DIM
    w_main, w_tail = _split_w_in(w_in, conv_dim + v_dim)
    proj, tail = rms_mm(x, g, w_main, w_tail)
    ng = nv // GDN_GROUP_V
    gates = jnp.concatenate([
        tail[:, :nv].reshape(m, ng, GDN_GROUP_V), tail[:, nv:2 * nv].reshape(m, ng, GDN_GROUP_V),
        jnp.zeros((m, ng, LANE - 2 * GDN_GROUP_V), F32)], axis=-1).transpose(1, 0, 2)
    pad = ((0, 0), (GDN_GROUP_V, LANE - 2 * GDN_GROUP_V))
    gate_par = jnp.stack([jnp.pad(a_log.reshape(ng, GDN_GROUP_V), pad),
                          jnp.pad(dt_bias.reshape(ng, GDN_GROUP_V), pad)], axis=1)
    o, s_p, *conv_p = gdn_core(proj, gates, conv_w, gate_par, norm_g, None, None, o_buf,
                               row0=0, nb=bp, t=tp, c=min(GDN_CHUNK, tp), n_v_heads=nv)
    o, s_s, *conv_s = gdn_core(proj, gates, conv_w, gate_par, norm_g, conv_buf, s0, o,
                               row0=mp, nb=bs, t=ts, c=ts, n_v_heads=nv)
    x = _out_proj(o, w_out, x)
    return x, (jnp.concatenate(conv_p, axis=-1), s_p), (jnp.concatenate(conv_s, axis=-1), s_s), o


def _fox_layer(x, g, w_in, b_f, w_out, k_cache, v_cache, lyr, lf_past, dims, o_buf):
    bp, tp, bs, ts = dims
    mp = bp * tp
    nh = b_f.shape[0]
    hd = FOX_HEAD_DIM
    fd = nh * hd
    npast = k_cache.shape[2]
    w_main, w_tail = _split_w_in(w_in, 3 * fd)
    proj, tail = rms_mm(x, g, w_main, w_tail)
    bias = _lanes(b_f, 0)
    lf_p, c_p = fox_gate(tail, bias, None, row0=0, nb=bp, t=tp)
    c_past = fox_gate(lf_past.reshape(bs * npast, nh), None, None, row0=0, nb=bs, t=npast)
    c_past = c_past.reshape(bs, npast, nh)
    init = jnp.pad(c_past[:, npast - 1:, :], ((0, 0), (0, 0), (0, LANE - nh)))
    lf_s, c_s = fox_gate(tail, bias, init, row0=mp, nb=bs, t=ts)

    def layouts(c, b, t):
        ct = c[:, :nh].reshape(b, t, nh).transpose(0, 2, 1)
        return jnp.broadcast_to(ct[..., None], (b, nh, t, LANE)), ct[:, :, None, :]

    cq_p, ck_p = layouts(c_p, bp, tp)
    cq_s, ck_s = layouts(c_s, bs, ts)
    ck_past = c_past.transpose(0, 2, 1)[:, :, None, :]
    o = fox_flash(proj, cq_p, ck_p, o_buf, nb=bp, t=tp, n_heads=nh, blk=min(FOX_BLOCK, tp))
    o, k_s, v_s = fox_sample(proj, k_cache, v_cache, lyr, cq_s, ck_past, ck_s, o,
                             row0=mp, nb=bs, t=ts, n_heads=nh)
    x = _out_proj(o, w_out, x)
    res_p = (proj[:mp, fd:2 * fd].reshape(bp, tp, nh, hd), proj[:mp, 2 * fd:3 * fd].reshape(bp, tp, nh, hd),
             lf_p[:, :nh].reshape(bp, tp, nh))
    res_s = (k_s.reshape(bs, ts, nh, hd), v_s.reshape(bs, ts, nh, hd), lf_s[:, :nh].reshape(bs, ts, nh))
    return x, res_p, res_s, o


def _gla_layer(x, g, w_in, w_gate, b_gate, norm_g, w_out, s0, dims, o_buf):
    bp, tp, bs, ts = dims
    mp = bp * tp
    nh = s0.shape[1]
    kd = w_gate.shape[1]
    w_main, w_tail = _split_w_in(w_in, 6 * kd)
    proj, glr = rms_mm(x, g, w_main, w_tail)
    wg = jnp.pad(w_gate, ((0, LANE - w_gate.shape[0]), (0, 0))).astype(BF16)
    o, s_p = gla_core(proj, glr, wg, b_gate, norm_g, None, o_buf,
                      row0=0, nb=bp, t=tp, c=min(GLA_CHUNK, tp), n_heads=nh)
    o, s_s = gla_core(proj, glr, wg, b_gate, norm_g, s0, o, row0=mp, nb=bs, t=ts, c=ts, n_heads=nh)
    x = _out_proj(o, w_out, x)
    return x, s_p, s_s, o


def kernel(x_prompt, x_sample, mem_prompt, state_gdn_conv, state_gdn_s, cache_fox_k, cache_fox_v, cache_fox_logf, state_gla_s, cache_mem_k, cache_mem_v, norm_mix, norm_mem, norm_memsrc, norm_ffn, norm_final, gdn_w_in, gdn_conv_w, gdn_a_log, gdn_dt_bias, gdn_norm, gdn_w_out, fox_w_in, fox_b_f, fox_w_out, gla_w_in, gla_w_gate, gla_b_gate, gla_norm, gla_w_out, mem_w_q, mem_w_kv, mem_w_o, ffn_w_up, ffn_w_down):
    bp, tp, d = x_prompt.shape
    bs, ts, _ = x_sample.shape
    dims = (bp, tp, bs, ts)
    mp, ms = bp * tp, bs * ts
    depth = norm_mix.shape[0]
    mem_len = mem_prompt.shape[1]
    mem_dim = mem_w_q.shape[2]
    x = jnp.concatenate([x_prompt.reshape(mp, d), x_sample.reshape(ms, d)], axis=0)
    mem = mem_prompt.reshape(bp * mem_len, d)
    w_up, w_down = ffn_w_up.astype(BF16), ffn_w_down.astype(BF16)
    w_gdn_out, w_fox_out, w_gla_out = gdn_w_out.astype(BF16), fox_w_out.astype(BF16), gla_w_out.astype(BF16)

    o_buf = jnp.zeros((mp + ms, gdn_w_out.shape[1]), BF16)
    assert fox_w_out.shape[1] == gla_w_out.shape[1] == gdn_w_out.shape[1]

    gdn_p, gdn_s, fox_p, fox_s, gla_p, gla_s, kv_p = [], [], [], [], [], [], []
    for i in range(depth):
        j, kind = divmod(i, 3)
        if kind == 0:
            x, res_p, res_s, o_buf = _gdn_layer(
                x, norm_mix[i], gdn_w_in[j], gdn_conv_w[j], gdn_a_log[j], gdn_dt_bias[j], gdn_norm[j],
                (w_gdn_out, j), state_gdn_conv[j], state_gdn_s[j], dims, o_buf)
            gdn_p.append(res_p)
            gdn_s.append(res_s)
        elif kind == 1:
            x, res_p, res_s, o_buf = _fox_layer(
                x, norm_mix[i], fox_w_in[j], fox_b_f[j], (w_fox_out, j), cache_fox_k, cache_fox_v, j,
                cache_fox_logf[j], dims, o_buf)
            fox_p.append(res_p)
            fox_s.append(res_s)
        else:
            x, res_p, res_s, o_buf = _gla_layer(
                x, norm_mix[i], gla_w_in[j], gla_w_gate[j], gla_b_gate[j], gla_norm[j], (w_gla_out, j),
                state_gla_s[j], dims, o_buf)
            gla_p.append(res_p)
            gla_s.append(res_s)

        kv = rms_mm(mem, norm_memsrc[i], mem_w_kv[i].astype(BF16)).reshape(bp, mem_len, 2 * mem_dim)
        kv_p.append(kv)
        wq = mem_w_q[i].astype(BF16)
        wo = mem_w_o[i].astype(BF16)
        x = mem_attn(x, norm_mem[i], wq, kv, kv, wo, row0=0, nb=bp, t=tp, tq=min(256, tp),
                     kmap=lambda b, q: (b, 0, 0), vmap=lambda b, q: (b, 0, 1), kblk=mem_dim)
        x = mem_attn(x, norm_mem[i], wq, cache_mem_k[i].reshape(bs, mem_len, mem_dim),
                     cache_mem_v[i].reshape(bs, mem_len, mem_dim), wo, row0=mp, nb=bs, t=ts, tq=ts,
                     kmap=lambda b, q: (b, 0, 0), vmap=lambda b, q: (b, 0, 0), kblk=mem_dim)

        u = rms_mm(x, norm_ffn[i], w_up, lyr=i, act="relu2", out_dtype=BF16)
        x = mm_res(u, w_down, x, lyr=i, tm=MM_TM, tn=MM_TN, tk=MM_TK)

    y_p, y_s = rmsnorm_split(x, norm_final, mp)
    kv_all = jnp.stack(kv_p)
    mem_shape = (depth, bp, mem_len, MEM_HEADS, MEM_HEAD_DIM)
    stack = lambda items, k: jnp.stack([it[k] for it in items])
    return (
        y_p.reshape(bp, tp, d), y_s.reshape(bs, ts, d),
        stack(gdn_p, 0), stack(gdn_p, 1), stack(fox_p, 0), stack(fox_p, 1), stack(fox_p, 2), jnp.stack(gla_p),
        kv_all[..., :mem_dim].reshape(mem_shape), kv_all[..., mem_dim:].reshape(mem_shape),
        stack(gdn_s, 0), stack(gdn_s, 1), stack(fox_s, 0), stack(fox_s, 1), stack(fox_s, 2), jnp.stack(gla_s),
    )
```

```python
import functools
import math

import jax
import jax.numpy as jnp
from jax import lax
from jax.experimental import pallas as pl
from jax.experimental.pallas import tpu as pltpu

F32 = jnp.float32
BF16 = jnp.bfloat16
NORM_EPS = 1e-6
LANE = 128
CONV_WIDTH = 4
GDN_HEAD_DIM = 128
FOX_HEAD_DIM = 128
FOX_HEADS = 32
GLA_HEADS = 4
GLA_TAU = 16.0
GLA_SUB = 16
MEM_HEADS = 4
MEM_HEAD_DIM = 128
VMEM_LIMIT = 56 * 1024 * 1024

_NT = (((1,), (1,)), ((), ()))
_TN = (((0,), (0,)), ((), ()))


def _params(*sem):
    return pltpu.CompilerParams(dimension_semantics=sem, vmem_limit_bytes=VMEM_LIMIT)


def _pick(n, pref):
    t = pref
    while n % t:
        assert t % 2 == 0 and t > 8, (n, pref)
        t //= 2
    return t


def _dot(a, b, dims=None):
    a = a.astype(BF16)
    b = b.astype(BF16)
    if dims is None:
        return jnp.dot(a, b, preferred_element_type=F32)
    return lax.dot_general(a, b, dims, preferred_element_type=F32)


def _split3(x):
    hi = x.astype(BF16)
    r = x - hi.astype(F32)
    mid = r.astype(BF16)
    lo = (r - mid.astype(F32)).astype(BF16)
    return hi, mid, lo


def _cumsum_rows(x):
    n = x.shape[0]
    tri = (lax.broadcasted_iota(jnp.int32, (n, n), 1) <= lax.broadcasted_iota(jnp.int32, (n, n), 0))
    tri = jnp.where(tri, 1.0, 0.0).astype(BF16)
    hi, mid, lo = _split3(x)
    out = jnp.dot(tri, lo, preferred_element_type=F32)
    out = out + jnp.dot(tri, mid, preferred_element_type=F32)
    return out + jnp.dot(tri, hi, preferred_element_type=F32)


def _log_sigmoid(x):
    return jnp.minimum(x, 0.0) - jnp.log1p(jnp.exp(-jnp.abs(x)))


def _softplus(x):
    return jnp.maximum(x, 0.0) + jnp.log1p(jnp.exp(-jnp.abs(x)))


def _sigmoid(x):
    return 1.0 / (1.0 + jnp.exp(-x))


def _silu(x):
    return x * _sigmoid(x)


def _rms_rows(x, g):
    return x * lax.rsqrt(jnp.mean(x * x, axis=-1, keepdims=True) + NORM_EPS) * g


def _rms_mm_kernel(*refs, act, has_tail):
    if has_tail:
        x_ref, g_ref, w_ref, wt_ref, o_ref, ot_ref, h_ref = refs
    else:
        x_ref, g_ref, w_ref, o_ref, h_ref = refs

    @pl.when(pl.program_id(1) == 0)
    def _():
        h_ref[...] = _rms_rows(x_ref[...], g_ref[...]).astype(BF16)
        if has_tail:
            ot_ref[...] = jnp.dot(h_ref[...], wt_ref[...], preferred_element_type=F32)

    acc = jnp.dot(h_ref[...], w_ref[...], preferred_element_type=F32)
    if act == "relu2":
        acc = jnp.square(jnp.maximum(acc, 0.0))
    o_ref[...] = acc.astype(o_ref.dtype)


def rms_mm(x, g, w, w_tail=None, *, lyr=None, act=None, out_dtype=F32, tm=512, tn=1024):
    m, k = x.shape
    n = w.shape[-1]
    tm, tn = _pick(m, tm), _pick(n, tn)
    has_tail = w_tail is not None
    if lyr is None:
        w_spec = pl.BlockSpec((k, tn), lambda i, j: (0, j))
    else:
        w_spec = pl.BlockSpec((None, k, tn), lambda i, j: (lyr, 0, j))
    in_specs = [
        pl.BlockSpec((tm, k), lambda i, j: (i, 0)),
        pl.BlockSpec((1, k), lambda i, j: (0, 0)),
        w_spec,
    ]
    out_specs = [pl.BlockSpec((tm, tn), lambda i, j: (i, j))]
    out_shape = [jax.ShapeDtypeStruct((m, n), out_dtype)]
    args = [x, g.reshape(1, k), w]
    if has_tail:
        in_specs.append(pl.BlockSpec((k, LANE), lambda i, j: (0, 0)))
        out_specs.append(pl.BlockSpec((tm, LANE), lambda i, j: (i, 0)))
        out_shape.append(jax.ShapeDtypeStruct((m, LANE), F32))
        args.append(w_tail)
    res = pl.pallas_call(
        functools.partial(_rms_mm_kernel, act=act, has_tail=has_tail),
        grid=(m // tm, n // tn),
        in_specs=in_specs,
        out_specs=out_specs,
        out_shape=out_shape,
        scratch_shapes=[pltpu.VMEM((tm, k), BF16)],
        compiler_params=_params("parallel", "arbitrary"),
        name="rms_mm",
    )(*args)
    return res if has_tail else res[0]


def _mm_res_kernel(a_ref, w_ref, r_ref, o_ref):
    @pl.when(pl.program_id(2) == 0)
    def _():
        o_ref[...] = r_ref[...]

    o_ref[...] += jnp.dot(a_ref[...], w_ref[...], preferred_element_type=F32)


def mm_res(a, w, res, *, lyr=None, tm=512, tn=512, tk=None):
    m, k = a.shape
    n = w.shape[-1]
    tm, tn, tk = _pick(m, tm), _pick(n, tn), _pick(k, k if tk is None else tk)
    if lyr is None:
        w_spec = pl.BlockSpec((tk, tn), lambda i, j, l: (l, j))
    else:
        w_spec = pl.BlockSpec((None, tk, tn), lambda i, j, l: (lyr, l, j))
    return pl.pallas_call(
        _mm_res_kernel,
        grid=(m // tm, n // tn, k // tk),
        in_specs=[
            pl.BlockSpec((tm, tk), lambda i, j, l: (i, l)),
            w_spec,
            pl.BlockSpec((tm, tn), lambda i, j, l: (i, j)),
        ],
        out_specs=pl.BlockSpec((tm, tn), lambda i, j, l: (i, j)),
        out_shape=jax.ShapeDtypeStruct((m, n), F32),
        input_output_aliases={2: 0},
        compiler_params=_params("parallel", "parallel", "arbitrary"),
        name="mm_res",
    )(a, w, res)


def _rmsnorm_split_kernel(x_ref, g_ref, a_ref, b_ref, *, na):
    y = _rms_rows(x_ref[...], g_ref[...])

    @pl.when(pl.program_id(0) < na)
    def _():
        a_ref[...] = y

    @pl.when(pl.program_id(0) >= na)
    def _():
        b_ref[...] = y


def rmsnorm_split(x, g, ma, *, tm=512):
    m, k = x.shape
    mb = m - ma
    tm = _pick(math.gcd(ma, mb), tm)
    na = ma // tm
    return pl.pallas_call(
        functools.partial(_rmsnorm_split_kernel, na=na),
        grid=(m // tm,),
        in_specs=[pl.BlockSpec((tm, k), lambda i: (i, 0)), pl.BlockSpec((1, k), lambda i: (0, 0))],
        out_specs=[pl.BlockSpec((tm, k), lambda i: (jnp.minimum(i, na - 1), 0)),
                   pl.BlockSpec((tm, k), lambda i: (jnp.maximum(i - na, 0), 0))],
        out_shape=[jax.ShapeDtypeStruct((ma, k), F32), jax.ShapeDtypeStruct((mb, k), F32)],
        compiler_params=_params("arbitrary"),
        name="rmsnorm_split",
    )(x, g.reshape(1, k))


def _mem_attn_kernel(x_ref, g_ref, wq_ref, k_ref, v_ref, wo_ref, o_ref):
    x = x_ref[...]
    h = _rms_rows(x, g_ref[...]).astype(BF16)
    q = jnp.dot(h, wq_ref[...], preferred_element_type=F32)
    k = k_ref[...].astype(BF16)
    v = v_ref[...].astype(BF16)
    outs = []
    for hh in range(MEM_HEADS):
        sl = slice(hh * MEM_HEAD_DIM, (hh + 1) * MEM_HEAD_DIM)
        s = _dot(q[:, sl], k[:, sl], _NT) * MEM_HEAD_DIM ** -0.5
        p = jnp.exp(s - jnp.max(s, axis=-1, keepdims=True))
        p = p / jnp.sum(p, axis=-1, keepdims=True)
        outs.append(_dot(p, v[:, sl]))
    o = jnp.concatenate(outs, axis=1).astype(BF16)
    o_ref[...] = x + jnp.dot(o, wo_ref[...], preferred_element_type=F32)


def mem_attn(x, g, wq, k, v, wo, *, row0, nb, t, tq, kmap, vmap, kblk):
    m, d = x.shape
    nq = t // tq
    r0 = row0 // tq
    dm = wq.shape[1]
    mem_len = k.shape[1]
    xmap = lambda b, i: (r0 + b * nq + i, 0)
    const = lambda b, i: (0, 0)
    return pl.pallas_call(
        _mem_attn_kernel,
        grid=(nb, nq),
        in_specs=[
            pl.BlockSpec((tq, d), xmap),
            pl.BlockSpec((1, d), const),
            pl.BlockSpec((d, dm), const, pipeline_mode=pl.Buffered(1)),
            pl.BlockSpec((None, mem_len, kblk), kmap),
            pl.BlockSpec((None, mem_len, kblk), vmap),
            pl.BlockSpec((dm, d), const, pipeline_mode=pl.Buffered(1)),
        ],
        out_specs=pl.BlockSpec((tq, d), xmap),
        out_shape=jax.ShapeDtypeStruct((m, d), F32),
        input_output_aliases={0: 0},
        compiler_params=_params("parallel", "parallel"),
        name="mem_attn",
    )(x, g.reshape(1, d), wq, k, v, wo)


def _fox_gate_kernel(*refs, gate, has_init):
    refs = list(refs)
    x_ref = refs.pop(0)
    bias_ref = refs.pop(0) if gate else None
    init_ref = refs.pop(0) if has_init else None
    lf_ref = refs.pop(0) if gate else None
    c_ref, carry_ref = refs

    @pl.when(pl.program_id(1) == 0)
    def _():
        carry_ref[...] = init_ref[...] if has_init else jnp.zeros_like(carry_ref)

    x = x_ref[...]
    if gate:
        x = _log_sigmoid(x + bias_ref[...])
        lf_ref[...] = x
    c = _cumsum_rows(x) + carry_ref[...]
    c_ref[...] = c
    carry_ref[...] = c[-1:, :]


def fox_gate(x, bias, init, *, row0, nb, t):
    w = x.shape[1]
    tb = min(t, 256)
    nt = t // tb
    r0 = row0 // tb
    gate = bias is not None
    has_init = init is not None
    xmap = lambda b, i: (r0 + b * nt + i, 0)
    omap = lambda b, i: (b * nt + i, 0)
    in_specs = [pl.BlockSpec((tb, w), xmap)]
    args = [x]
    if gate:
        in_specs.append(pl.BlockSpec((1, w), lambda b, i: (0, 0)))
        args.append(bias)
    if has_init:
        in_specs.append(pl.BlockSpec((None, 1, w), lambda b, i: (b, 0, 0)))
        args.append(init)
    n_out = 2 if gate else 1
    res = pl.pallas_call(
        functools.partial(_fox_gate_kernel, gate=gate, has_init=has_init),
        grid=(nb, nt),
        in_specs=in_specs,
        out_specs=[pl.BlockSpec((tb, w), omap)] * n_out,
        out_shape=[jax.ShapeDtypeStruct((nb * t, w), F32)] * n_out,
        scratch_shapes=[pltpu.VMEM((1, w), F32)],
        compiler_params=_params("parallel", "arbitrary"),
        name="fox_gate",
    )(*args)
    return res if gate else res[0]


def _fox_flash_kernel(q_ref, k_ref, v_ref, cq_ref, ck_ref, prev_ref, o_ref, *, blk):
    del prev_ref
    t = q_ref.shape[0]
    nq = t // blk
    q_b = q_ref[...].astype(BF16)
    k_b = k_ref[...].astype(BF16)
    v_b = v_ref[...].astype(BF16)
    log2e = 1.0 / math.log(2.0)
    cq = cq_ref[...] * log2e
    ck = ck_ref[...] * log2e
    row = lax.broadcasted_iota(jnp.int32, (blk, blk), 0)
    col = lax.broadcasted_iota(jnp.int32, (blk, blk), 1)
    m, l, acc = [None] * nq, [None] * nq, [None] * nq
    for j in range(nq):
        ks = slice(j * blk, (j + 1) * blk)
        for i in range(j, nq):
            qs = slice(i * blk, (i + 1) * blk)
            s = lax.dot_general(q_b[qs], k_b[ks], _NT, preferred_element_type=F32) * (FOX_HEAD_DIM ** -0.5 * log2e)
            s = s + (jnp.concatenate([cq[qs]] * (blk // LANE), axis=1) - ck[:, ks])
            if i == j:
                s = jnp.where(col <= row, s, -jnp.inf)
            s_max = jnp.max(s, axis=-1, keepdims=True)
            if j == 0:
                m[i] = s_max
                p = jnp.exp2(s - s_max)
                l[i] = jnp.sum(p, axis=-1, keepdims=True)
                acc[i] = jnp.dot(p.astype(BF16), v_b[ks], preferred_element_type=F32)
            else:
                m_new = jnp.maximum(m[i], s_max)
                alpha = jnp.exp2(m[i] - m_new)
                p = jnp.exp2(s - m_new)
                l[i] = alpha * l[i] + jnp.sum(p, axis=-1, keepdims=True)
                acc[i] = alpha * acc[i] + jnp.dot(p.astype(BF16), v_b[ks], preferred_element_type=F32)
                m[i] = m_new
    o_ref[...] = jnp.concatenate([acc[i] / l[i] for i in range(nq)], axis=0).astype(o_ref.dtype)


def fox_flash(proj, cq_b, ck_row, o_prev, *, nb, t, n_heads, blk=512):
    assert t % blk == 0 and blk % LANE == 0
    d = FOX_HEAD_DIM
    return pl.pallas_call(
        functools.partial(_fox_flash_kernel, blk=blk),
        grid=(nb, n_heads),
        in_specs=[
            pl.BlockSpec((t, d), lambda b, h: (b, h)),
            pl.BlockSpec((t, d), lambda b, h: (b, n_heads + h)),
            pl.BlockSpec((t, d), lambda b, h: (b, 2 * n_heads + h)),
            pl.BlockSpec((None, None, t, LANE), lambda b, h: (b, h, 0, 0)),
            pl.BlockSpec((None, None, 1, t), lambda b, h: (b, h, 0, 0)),
            pl.BlockSpec(memory_space=pl.ANY),
        ],
        out_specs=pl.BlockSpec((t, d), lambda b, h: (b, h)),
        out_shape=jax.ShapeDtypeStruct(o_prev.shape, o_prev.dtype),
        input_output_aliases={5: 0},
        compiler_params=_params("parallel", "parallel"),
        name="fox_flash",
    )(proj, proj, proj, cq_b, ck_row, o_prev)


FOX_SAMPLE_GROUP = 8


def _fox_sample_kernel(q_ref, kn_ref, vn_ref, kc_hbm, vc_hbm, cq_ref, ckp_ref, ckn_ref, prev_ref,
                       o_ref, ko_ref, vo_ref, kbuf, vbuf, sem, *, lyr):
    del prev_ref
    t = q_ref.shape[0]
    d = FOX_HEAD_DIM
    hg = FOX_SAMPLE_GROUP
    scale = d ** -0.5
    npast = kbuf.shape[2]
    ng = pl.num_programs(1)
    step = pl.program_id(0) * ng + pl.program_id(1)
    slot = lax.rem(step, 2)

    def past_copies(s, into):
        bb = lax.div(s, ng)
        h0 = lax.rem(s, ng) * hg
        cps = []
        for hh in range(hg):
            cps.append(pltpu.make_async_copy(kc_hbm.at[lyr, bb, :, h0 + hh, :], kbuf.at[into, hh], sem.at[into, 0]))
            cps.append(pltpu.make_async_copy(vc_hbm.at[lyr, bb, :, h0 + hh, :], vbuf.at[into, hh], sem.at[into, 1]))
        return cps

    @pl.when(step == 0)
    def _():
        for cp in past_copies(step, slot):
            cp.start()

    @pl.when(step + 1 < pl.num_programs(0) * ng)
    def _():
        for cp in past_copies(step + 1, 1 - slot):
            cp.start()

    for cp in past_copies(step, slot):
        cp.wait()

    row = lax.broadcasted_iota(jnp.int32, (t, t), 0)
    col = lax.broadcasted_iota(jnp.int32, (t, t), 1)
    outs = []
    for hh in range(hg):
        sl = slice(hh * d, (hh + 1) * d)
        q = q_ref[:, sl]
        kn = kn_ref[:, sl]
        vn = vn_ref[:, sl]
        ko_ref[:, hh, :] = kn
        vo_ref[:, hh, :] = vn
        cq = cq_ref[hh]
        cq_wide = jnp.concatenate([cq] * (npast // LANE), axis=1)
        s_p = _dot(q, kbuf[slot, hh], _NT) * scale + (cq_wide - ckp_ref[hh])
        s_n = _dot(q, kn, _NT) * scale + (cq[:, :t] - ckn_ref[hh])
        s_n = jnp.where(col <= row, s_n, -jnp.inf)
        m = jnp.maximum(jnp.max(s_p, axis=-1, keepdims=True), jnp.max(s_n, axis=-1, keepdims=True))
        p_p = jnp.exp(s_p - m)
        p_n = jnp.exp(s_n - m)
        inv = 1.0 / (jnp.sum(p_p, axis=-1, keepdims=True) + jnp.sum(p_n, axis=-1, keepdims=True))
        outs.append(_dot(p_p * inv, vbuf[slot, hh]) + _dot(p_n * inv, vn))
    o_ref[...] = jnp.concatenate(outs, axis=1).astype(o_ref.dtype)


def fox_sample(proj, k_cache, v_cache, lyr, cq_b, ck_past, ck_new, o_prev, *, row0, nb, t, n_heads):
    d = FOX_HEAD_DIM
    hg = FOX_SAMPLE_GROUP
    ng = n_heads // hg
    r0 = row0 // t
    npast = k_cache.shape[2]
    bias = lambda b, g: (b, g, 0, 0)
    return pl.pallas_call(
        functools.partial(_fox_sample_kernel, lyr=lyr),
        grid=(nb, ng),
        in_specs=[
            pl.BlockSpec((t, hg * d), lambda b, g: (r0 + b, g)),
            pl.BlockSpec((t, hg * d), lambda b, g: (r0 + b, ng + g)),
            pl.BlockSpec((t, hg * d), lambda b, g: (r0 + b, 2 * ng + g)),
            pl.BlockSpec(memory_space=pl.ANY),
            pl.BlockSpec(memory_space=pl.ANY),
            pl.BlockSpec((None, hg, t, LANE), bias),
            pl.BlockSpec((None, hg, 1, npast), bias),
            pl.BlockSpec((None, hg, 1, t), bias),
            pl.BlockSpec(memory_space=pl.ANY),
        ],
        out_specs=[pl.BlockSpec((t, hg * d), lambda b, g: (r0 + b, g)),
                   pl.BlockSpec((t, hg, d), lambda b, g: (b, g, 0)),
                   pl.BlockSpec((t, hg, d), lambda b, g: (b, g, 0))],
        out_shape=[
            jax.ShapeDtypeStruct(o_prev.shape, o_prev.dtype),
            jax.ShapeDtypeStruct((nb * t, n_heads, d), F32),
            jax.ShapeDtypeStruct((nb * t, n_heads, d), F32),
        ],
        scratch_shapes=[pltpu.VMEM((2, hg, npast, d), F32), pltpu.VMEM((2, hg, npast, d), F32),
                        pltpu.SemaphoreType.DMA((2, 2))],
        input_output_aliases={8: 0},
        compiler_params=_params("arbitrary", "arbitrary"),
        name="fox_sample",
    )(proj, proj, proj, k_cache, v_cache, cq_b, ck_past, ck_new, o_prev)


GDN_GROUP_V = 16
GDN_INV_BASE = 16


def _transpose_lanes(x):
    eye = (lax.broadcasted_iota(jnp.int32, (LANE, LANE), 0) == lax.broadcasted_iota(jnp.int32, (LANE, LANE), 1))
    eye = jnp.where(eye, 1.0, 0.0).astype(BF16)
    hi, mid, lo = _split3(x)
    out = lax.dot_general(eye, lo, _NT, preferred_element_type=F32)
    out = out + lax.dot_general(eye, mid, _NT, preferred_element_type=F32)
    return out + lax.dot_general(eye, hi, _NT, preferred_element_type=F32)


def _unit_lower_inverses(mats):
    n = mats[0].shape[0]
    row = lax.broadcasted_iota(jnp.int32, (n, n), 0)
    col = lax.broadcasted_iota(jnp.int32, (n, n), 1)
    blk = min(GDN_INV_BASE, n)
    shift = int(math.log2(blk))
    same = (row >> shift) == (col >> shift)
    eye = jnp.where(row == col, 1.0, 0.0).astype(F32)
    ps = [jnp.where(same, a, 0.0).astype(BF16) for a in mats]
    ts = [eye - p.astype(F32) for p in ps]
    for _ in range(shift - 1):
        ps = [jnp.dot(p, p, preferred_element_type=F32).astype(BF16) for p in ps]
        ts = [t + jnp.dot(t.astype(BF16), p, preferred_element_type=F32) for t, p in zip(ts, ps)]
    while blk < n:
        blk *= 2
        shift += 1
        same2 = (row >> shift) == (col >> shift)
        off = jnp.logical_and(same2, jnp.logical_not(same))
        ms = [jnp.where(off, a, 0.0).astype(BF16) for a in mats]
        tbs = [t.astype(BF16) for t in ts]
        tms = [jnp.dot(tb, m, preferred_element_type=F32).astype(BF16) for tb, m in zip(tbs, ms)]
        ts = [t - jnp.dot(tm, tb, preferred_element_type=F32) for t, tm, tb in zip(ts, tms, tbs)]
        same = same2
    return ts


def _gdn_kernel(*refs, c, has_state):
    refs = list(refs)
    q_ref, k_ref, v_ref, z_ref, gate_ref, wq_ref, wk_ref, wv_ref, gp_ref, ng_ref = refs[:10]
    refs = refs[10:]
    if has_state:
        bq_ref, bk_ref, bv_ref, s0_ref = refs[:4]
        refs = refs[4:]
    _prev, o_ref, s_ref, cq_ref, ck_ref, cv_ref, xq_ref, xk_ref, xv_ref = refs
    d = GDN_HEAD_DIM

    @pl.when(pl.program_id(2) == 0)
    def _():
        for x_ref, b_ref in ((xq_ref, bq_ref if has_state else None), (xk_ref, bk_ref if has_state else None),
                             (xv_ref, bv_ref if has_state else None)):
            x_ref[0:8, :] = jnp.zeros((8, x_ref.shape[1]), F32)
            if has_state:
                x_ref[8 - (CONV_WIDTH - 1):8, :] = b_ref[...]
        s_ref[...] = s0_ref[...] if has_state else jnp.zeros_like(s_ref)

    def conv(x_ref, raw_ref, w_ref):
        x_ref[8:8 + c, :] = raw_ref[...]
        w = w_ref[...]
        out = x_ref[8:8 + c, :] * w[3:4, :]
        for i in range(CONV_WIDTH - 1):
            out = out + x_ref[5 + i:5 + i + c, :] * w[i:i + 1, :]
        x_ref[0:8, :] = x_ref[c:c + 8, :]
        return _silu(out)

    qc = conv(xq_ref, q_ref, wq_ref)
    kc = conv(xk_ref, k_ref, wk_ref)
    vc = conv(xv_ref, v_ref, wv_ref)

    gt = gate_ref[...]
    gp = gp_ref[...]
    gv = GDN_GROUP_V
    beta_all = _sigmoid(gt)
    g_all = -jnp.exp(gp[0:1, :]) * _softplus(gt + gp[1:2, :])
    gc_all = _cumsum_rows(g_all)
    gc_t = _transpose_lanes(gc_all)

    row = lax.broadcasted_iota(jnp.int32, (c, c), 0)
    col = lax.broadcasted_iota(jnp.int32, (c, c), 1)
    incl = col <= row
    strict = col < row
    ng = ng_ref[...]
    z = z_ref[...]

    heads = range(GDN_GROUP_V)
    q_hs, k_hs, kks, qks = [], [], [], []
    for hq in range(GDN_GROUP_V // 2):
        sl = slice(hq * d, (hq + 1) * d)
        q_h = qc[:, sl]
        k_h = kc[:, sl]
        q_h = q_h * lax.rsqrt(jnp.sum(q_h * q_h, axis=-1, keepdims=True) + NORM_EPS) * d ** -0.5
        k_h = k_h * lax.rsqrt(jnp.sum(k_h * k_h, axis=-1, keepdims=True) + NORM_EPS)
        k_b = k_h.astype(BF16)
        q_hs.append(q_h)
        k_hs.append(k_h)
        kks.append(lax.dot_general(k_b, k_b, _NT, preferred_element_type=F32))
        qks.append(lax.dot_general(q_h.astype(BF16), k_b, _NT, preferred_element_type=F32))

    betas = [beta_all[:, j:j + 1] for j in heads]
    gcols = [gc_all[:, gv + j:gv + j + 1] for j in heads]
    glasts = [gc_all[c - 1:c, gv + j:gv + j + 1] for j in heads]
    egs = [jnp.exp(gcols[j]) for j in heads]
    decays = [jnp.exp(jnp.where(incl, gcols[j] - gc_t[gv + j:gv + j + 1, :], -jnp.inf)) for j in heads]
    tinvs = _unit_lower_inverses([jnp.where(strict, betas[j] * kks[j // 2] * decays[j], 0.0) for j in heads])
    rhss = [jnp.concatenate([k_hs[j // 2] * (betas[j] * egs[j]), vc[:, j * d:(j + 1) * d] * betas[j]], axis=1)
            for j in heads]
    sols = [_dot(tinvs[j], rhss[j]) for j in heads]
    s_olds = [s_ref[j] for j in heads]
    wss = [_dot(jnp.concatenate([sols[j][:, :d], q_hs[j // 2] * egs[j]], axis=0), s_olds[j]) for j in heads]
    u_news = [sols[j][:, d:] - wss[j][:c] for j in heads]
    os_ = [wss[j][c:] + _dot(qks[j // 2] * decays[j], u_news[j]) for j in heads]
    s_news = [s_olds[j] * jnp.exp(glasts[j])
              + _dot(k_hs[j // 2] * jnp.exp(glasts[j] - gcols[j]), u_news[j], _TN) for j in heads]
    s_ref[...] = jnp.stack(s_news, axis=0)
    outs = [o * lax.rsqrt(jnp.mean(o * o, axis=-1, keepdims=True) + NORM_EPS) * ng for o in os_]
    o_ref[...] = (jnp.concatenate(outs, axis=1) * _silu(z)).astype(o_ref.dtype)

    @pl.when(pl.program_id(2) == pl.num_programs(2) - 1)
    def _():
        for c_ref, x_ref in ((cq_ref, xq_ref), (ck_ref, xk_ref), (cv_ref, xv_ref)):
            c_ref[...] = x_ref[8 - (CONV_WIDTH - 1):8, :]


def gdn_core(proj, gates, conv_w, gate_par, norm_g, conv_buf, s0, o_prev, *, row0, nb, t, c, n_v_heads):
    has_state = s0 is not None
    d = GDN_HEAD_DIM
    gv = GDN_GROUP_V
    ng = n_v_heads // gv
    wqk = (gv // 2) * d
    wv = gv * d
    nqk = (n_v_heads // 2) * d // wqk
    nc = t // c
    r0 = row0 // c
    m = proj.shape[0]
    rows = lambda b, g, i: r0 + b * nc + i
    in_specs = [
        pl.BlockSpec((c, wqk), lambda b, g, i: (rows(b, g, i), g)),
        pl.BlockSpec((c, wqk), lambda b, g, i: (rows(b, g, i), nqk + g)),
        pl.BlockSpec((c, wv), lambda b, g, i: (rows(b, g, i), ng + g)),
        pl.BlockSpec((c, wv), lambda b, g, i: (rows(b, g, i), 2 * ng + g)),
        pl.BlockSpec((None, c, LANE), lambda b, g, i: (g, rows(b, g, i), 0)),
        pl.BlockSpec((CONV_WIDTH, wqk), lambda b, g, i: (0, g)),
        pl.BlockSpec((CONV_WIDTH, wqk), lambda b, g, i: (0, nqk + g)),
        pl.BlockSpec((CONV_WIDTH, wv), lambda b, g, i: (0, ng + g)),
        pl.BlockSpec((None, 2, LANE), lambda b, g, i: (g, 0, 0)),
        pl.BlockSpec((1, d), lambda b, g, i: (0, 0)),
    ]
    args = [proj, proj, proj, proj, gates, conv_w, conv_w, conv_w, gate_par, norm_g.reshape(1, d)]
    if has_state:
        in_specs += [
            pl.BlockSpec((None, CONV_WIDTH - 1, wqk), lambda b, g, i: (b, 0, g)),
            pl.BlockSpec((None, CONV_WIDTH - 1, wqk), lambda b, g, i: (b, 0, nqk + g)),
            pl.BlockSpec((None, CONV_WIDTH - 1, wv), lambda b, g, i: (b, 0, ng + g)),
            pl.BlockSpec((None, gv, d, d), lambda b, g, i: (b, g, 0, 0)),
        ]
        args += [conv_buf, conv_buf, conv_buf, s0]
    in_specs.append(pl.BlockSpec(memory_space=pl.ANY))
    args.append(o_prev)
    aliases = {len(args) - 1: 0}
    return pl.pallas_call(
        functools.partial(_gdn_kernel, c=c, has_state=has_state),
        grid=(nb, ng, nc),
        in_specs=in_specs,
        out_specs=[
            pl.BlockSpec((c, wv), lambda b, g, i: (rows(b, g, i), g)),
            pl.BlockSpec((None, gv, d, d), lambda b, g, i: (b, g, 0, 0)),
            pl.BlockSpec((None, CONV_WIDTH - 1, wqk), lambda b, g, i: (b, 0, g)),
            pl.BlockSpec((None, CONV_WIDTH - 1, wqk), lambda b, g, i: (b, 0, g)),
            pl.BlockSpec((None, CONV_WIDTH - 1, wv), lambda b, g, i: (b, 0, g)),
        ],
        out_shape=[
            jax.ShapeDtypeStruct((m, n_v_heads * d), BF16),
            jax.ShapeDtypeStruct((nb, n_v_heads, d, d), F32),
            jax.ShapeDtypeStruct((nb, CONV_WIDTH - 1, nqk * wqk), F32),
            jax.ShapeDtypeStruct((nb, CONV_WIDTH - 1, nqk * wqk), F32),
            jax.ShapeDtypeStruct((nb, CONV_WIDTH - 1, ng * wv), F32),
        ],
        scratch_shapes=[pltpu.VMEM((c + 8, wqk), F32), pltpu.VMEM((c + 8, wqk), F32), pltpu.VMEM((c + 8, wv), F32)],
        input_output_aliases=aliases,
        compiler_params=_params("parallel", "parallel", "arbitrary"),
        name="gdn_core",
    )(*args)


def _gla_kernel(*refs, c, has_state):
    refs = list(refs)
    q_ref, k_ref, v_ref, r_ref, glr_ref, wg_ref, bg_ref, ng_ref = refs[:8]
    refs = refs[8:]
    if has_state:
        s0_ref = refs.pop(0)
    _prev, o_ref, s_ref, st_ref = refs
    dk = q_ref.shape[1]
    sub = min(GLA_SUB, c)
    ci = pl.program_id(2)

    @pl.when(ci == 0)
    def _():
        st_ref[...] = s0_ref[...].T if has_state else jnp.zeros_like(st_ref)

    glog = _log_sigmoid(_dot(glr_ref[...], wg_ref[...]) + bg_ref[...]) * (1.0 / GLA_TAU)
    bc = _cumsum_rows(glog)
    q = q_ref[...] * dk ** -0.5
    k = k_ref[...]
    v = v_ref[...].astype(BF16)
    st = st_ref[...]
    o = _dot(q * jnp.exp(bc), st, _NT)

    tok = lax.broadcasted_iota(jnp.int32, (c, 1), 0)
    a_rows = []
    for i in range(c // sub):
        lo = i * sub
        if i == 0:
            a_rows.append(jnp.zeros((sub, c), F32))
            continue
        ref_i = bc[lo:lo + 1, :]
        q_off = q[lo:lo + sub, :] * jnp.exp(bc[lo:lo + sub, :] - ref_i)
        k_off = k * jnp.exp(jnp.where(tok < lo, ref_i - bc, -jnp.inf))
        a_rows.append(_dot(q_off, k_off, _NT))
    a = jnp.concatenate(a_rows, axis=0)

    row = lax.broadcasted_iota(jnp.int32, (c, c), 0)
    col = lax.broadcasted_iota(jnp.int32, (c, c), 1)
    rmod = jnp.bitwise_and(row, sub - 1)
    for dlt in range(sub):
        k_r = k if dlt == 0 else pltpu.roll(k, dlt, 0)
        b_r = bc if dlt == 0 else pltpu.roll(bc, dlt, 0)
        e = jnp.exp(jnp.minimum(bc - b_r, 0.0))
        val = jnp.sum(q * k_r * e, axis=-1, keepdims=True)
        a = a + jnp.where(jnp.logical_and(col == row - dlt, rmod >= dlt), val, 0.0)

    o = o + _dot(a, v)
    b_last = bc[c - 1:c, :]
    k_tail = k * jnp.exp(b_last - bc)
    st_new = st * jnp.exp(b_last) + _dot(v, k_tail, _TN)
    st_ref[...] = st_new
    o = o * lax.rsqrt(jnp.mean(o * o, axis=-1, keepdims=True) + NORM_EPS) * ng_ref[...] * _silu(r_ref[...])
    o_ref[...] = o.astype(o_ref.dtype)

    @pl.when(ci == pl.num_programs(2) - 1)
    def _():
        s_ref[...] = st_new.T


def gla_core(proj, glr, w_gate, b_gate, norm_g, s0, o_prev, *, row0, nb, t, c, n_heads):
    has_state = s0 is not None
    m = proj.shape[0]
    dk = w_gate.shape[1] // n_heads
    dv = 2 * dk
    nc = t // c
    r0 = row0 // c
    rows = lambda b, h, i: r0 + b * nc + i
    in_specs = [
        pl.BlockSpec((c, dk), lambda b, h, i: (rows(b, h, i), h)),
        pl.BlockSpec((c, dk), lambda b, h, i: (rows(b, h, i), n_heads + h)),
        pl.BlockSpec((c, dv), lambda b, h, i: (rows(b, h, i), n_heads + h)),
        pl.BlockSpec((c, dv), lambda b, h, i: (rows(b, h, i), 2 * n_heads + h)),
        pl.BlockSpec((c, LANE), lambda b, h, i: (rows(b, h, i), 0)),
        pl.BlockSpec((LANE, dk), lambda b, h, i: (0, h)),
        pl.BlockSpec((1, dk), lambda b, h, i: (0, h)),
        pl.BlockSpec((1, dv), lambda b, h, i: (0, 0)),
    ]
    args = [proj, proj, proj, proj, glr, w_gate, b_gate.reshape(1, -1), norm_g.reshape(1, dv)]
    if has_state:
        in_specs.append(pl.BlockSpec((None, None, dk, dv), lambda b, h, i: (b, h, 0, 0)))
        args.append(s0)
    in_specs.append(pl.BlockSpec(memory_space=pl.ANY))
    args.append(o_prev)
    aliases = {len(args) - 1: 0}
    return pl.pallas_call(
        functools.partial(_gla_kernel, c=c, has_state=has_state),
        grid=(nb, n_heads, nc),
        in_specs=in_specs,
        out_specs=[
            pl.BlockSpec((c, dv), lambda b, h, i: (rows(b, h, i), h)),
            pl.BlockSpec((None, None, dk, dv), lambda b, h, i: (b, h, 0, 0)),
        ],
        out_shape=[
            jax.ShapeDtypeStruct((m, n_heads * dv), BF16),
            jax.ShapeDtypeStruct((nb, n_heads, dk, dv), F32),
        ],
        scratch_shapes=[pltpu.VMEM((dv, dk), F32)],
        input_output_aliases=aliases,
        compiler_params=_params("parallel", "parallel", "arbitrary"),
        name="gla_core",
    )(*args)


GDN_CHUNK = 128
GLA_CHUNK = 128
FOX_BLOCK = 512
MEM_TQ = 512
MM_TM, MM_TN, MM_TK = 1536, 1024, 1024


def _split_w_in(w_in, n_main):
    tail = w_in[:, n_main:]
    tail = jnp.pad(tail, ((0, 0), (0, LANE - tail.shape[1])))
    return w_in[:, :n_main].astype(BF16), tail.astype(BF16)


def _out_proj(o, w_out, x):
    if isinstance(w_out, tuple):
        return mm_res(o, w_out[0], x, lyr=w_out[1], tm=MM_TM, tn=MM_TN, tk=MM_TK)
    return mm_res(o, w_out.astype(BF16), x, tm=MM_TM, tn=MM_TN, tk=MM_TK)


def _lanes(v, start):
    return jnp.pad(v, (start, LANE - start - v.shape[0])).reshape(1, LANE)


def _gdn_layer(x, g, w_in, conv_w, a_log, dt_bias, norm_g, w_out, conv_buf, s0, dims, o_buf):
    bp, tp, bs, ts = dims
    mp, m = bp * tp, x.shape[0]
    nv = a_log.shape[0]
    conv_dim = conv_w.shape[1]
    v_dim = nv * GDN_HEAD_DIM
    w_main, w_tail = _split_w_in(w_in, conv_dim + v_dim)
    proj, tail = rms_mm(x, g, w_main, w_tail)
    ng = nv // GDN_GROUP_V
    gates = jnp.concatenate([
        tail[:, :nv].reshape(m, ng, GDN_GROUP_V), tail[:, nv:2 * nv].reshape(m, ng, GDN_GROUP_V),
        jnp.zeros((m, ng, LANE - 2 * GDN_GROUP_V), F32)], axis=-1).transpose(1, 0, 2)
    pad = ((0, 0), (GDN_GROUP_V, LANE - 2 * GDN_GROUP_V))
    gate_par = jnp.stack([jnp.pad(a_log.reshape(ng, GDN_GROUP_V), pad),
                          jnp.pad(dt_bias.reshape(ng, GDN_GROUP_V), pad)], axis=1)
    o, s_p, *conv_p = gdn_core(proj, gates, conv_w, gate_par, norm_g, None, None, o_buf,
                               row0=0, nb=bp, t=tp, c=min(GDN_CHUNK, tp), n_v_heads=nv)
    o, s_s, *conv_s = gdn_core(proj, gates, conv_w, gate_par, norm_g, conv_buf, s0, o,
                               row0=mp, nb=bs, t=ts, c=ts, n_v_heads=nv)
    x = _out_proj(o, w_out, x)
    return x, (jnp.concatenate(conv_p, axis=-1), s_p), (jnp.concatenate(conv_s, axis=-1), s_s), o


def _fox_layer(x, g, w_in, b_f, w_out, k_cache, v_cache, lyr, lf_past, dims, o_buf):
    bp, tp, bs, ts = dims
    mp = bp * tp
    nh = b_f.shape[0]
    hd = FOX_HEAD_DIM
    fd = nh * hd
    npast = k_cache.shape[2]
    w_main, w_tail = _split_w_in(w_in, 3 * fd)
    proj, tail = rms_mm(x, g, w_main, w_tail)
    bias = _lanes(b_f, 0)
    lf_p, c_p = fox_gate(tail, bias, None, row0=0, nb=bp, t=tp)
    c_past = fox_gate(lf_past.reshape(bs * npast, nh), None, None, row0=0, nb=bs, t=npast)
    c_past = c_past.reshape(bs, npast, nh)
    init = jnp.pad(c_past[:, npast - 1:, :], ((0, 0), (0, 0), (0, LANE - nh)))
    lf_s, c_s = fox_gate(tail, bias, init, row0=mp, nb=bs, t=ts)

    def layouts(c, b, t):
        ct = c[:, :nh].reshape(b, t, nh).transpose(0, 2, 1)
        return jnp.broadcast_to(ct[..., None], (b, nh, t, LANE)), ct[:, :, None, :]

    cq_p, ck_p = layouts(c_p, bp, tp)
    cq_s, ck_s = layouts(c_s, bs, ts)
    ck_past = c_past.transpose(0, 2, 1)[:, :, None, :]
    o = fox_flash(proj, cq_p, ck_p, o_buf, nb=bp, t=tp, n_heads=nh, blk=min(FOX_BLOCK, tp))
    o, k_s, v_s = fox_sample(proj, k_cache, v_cache, lyr, cq_s, ck_past, ck_s, o,
                             row0=mp, nb=bs, t=ts, n_heads=nh)
    x = _out_proj(o, w_out, x)
    res_p = (proj[:mp, fd:2 * fd].reshape(bp, tp, nh, hd), proj[:mp, 2 * fd:3 * fd].reshape(bp, tp, nh, hd),
             lf_p[:, :nh].reshape(bp, tp, nh))
    res_s = (k_s.reshape(bs, ts, nh, hd), v_s.reshape(bs, ts, nh, hd), lf_s[:, :nh].reshape(bs, ts, nh))
    return x, res_p, res_s, o


def _gla_layer(x, g, w_in, w_gate, b_gate, norm_g, w_out, s0, dims, o_buf):
    bp, tp, bs, ts = dims
    mp = bp * tp
    nh = s0.shape[1]
    kd = w_gate.shape[1]
    w_main, w_tail = _split_w_in(w_in, 6 * kd)
    proj, glr = rms_mm(x, g, w_main, w_tail)
    wg = jnp.pad(w_gate, ((0, LANE - w_gate.shape[0]), (0, 0))).astype(BF16)
    o, s_p = gla_core(proj, glr, wg, b_gate, norm_g, None, o_buf,
                      row0=0, nb=bp, t=tp, c=min(GLA_CHUNK, tp), n_heads=nh)
    o, s_s = gla_core(proj, glr, wg, b_gate, norm_g, s0, o, row0=mp, nb=bs, t=ts, c=ts, n_heads=nh)
    x = _out_proj(o, w_out, x)
    return x, s_p, s_s, o


def kernel(x_prompt, x_sample, mem_prompt, state_gdn_conv, state_gdn_s, cache_fox_k, cache_fox_v, cache_fox_logf, state_gla_s, cache_mem_k, cache_mem_v, norm_mix, norm_mem, norm_memsrc, norm_ffn, norm_final, gdn_w_in, gdn_conv_w, gdn_a_log, gdn_dt_bias, gdn_norm, gdn_w_out, fox_w_in, fox_b_f, fox_w_out, gla_w_in, gla_w_gate, gla_b_gate, gla_norm, gla_w_out, mem_w_q, mem_w_kv, mem_w_o, ffn_w_up, ffn_w_down):
    bp, tp, d = x_prompt.shape
    bs, ts, _ = x_sample.shape
    dims = (bp, tp, bs, ts)
    mp, ms = bp * tp, bs * ts
    depth = norm_mix.shape[0]
    mem_len = mem_prompt.shape[1]
    mem_dim = mem_w_q.shape[2]
    x = jnp.concatenate([x_prompt.reshape(mp, d), x_sample.reshape(ms, d)], axis=0)
    mem = mem_prompt.reshape(bp * mem_len, d)
    w_up, w_down = ffn_w_up.astype(BF16), ffn_w_down.astype(BF16)
    w_gdn_out, w_fox_out, w_gla_out = gdn_w_out.astype(BF16), fox_w_out.astype(BF16), gla_w_out.astype(BF16)

    o_buf = jnp.zeros((mp + ms, gdn_w_out.shape[1]), BF16)
    assert fox_w_out.shape[1] == gla_w_out.shape[1] == gdn_w_out.shape[1]

    gdn_p, gdn_s, fox_p, fox_s, gla_p, gla_s, kv_p = [], [], [], [], [], [], []
    for i in range(depth):
        j, kind = divmod(i, 3)
        if kind == 0:
            x, res_p, res_s, o_buf = _gdn_layer(
                x, norm_mix[i], gdn_w_in[j], gdn_conv_w[j], gdn_a_log[j], gdn_dt_bias[j], gdn_norm[j],
                (w_gdn_out, j), state_gdn_conv[j], state_gdn_s[j], dims, o_buf)
            gdn_p.append(res_p)
            gdn_s.append(res_s)
        elif kind == 1:
            x, res_p, res_s, o_buf = _fox_layer(
                x, norm_mix[i], fox_w_in[j], fox_b_f[j], (w_fox_out, j), cache_fox_k, cache_fox_v, j,
                cache_fox_logf[j], dims, o_buf)
            fox_p.append(res_p)
            fox_s.append(res_s)
        else:
            x, res_p, res_s, o_buf = _gla_layer(
                x, norm_mix[i], gla_w_in[j], gla_w_gate[j], gla_b_gate[j], gla_norm[j], (w_gla_out, j),
                state_gla_s[j], dims, o_buf)
            gla_p.append(res_p)
            gla_s.append(res_s)

        kv = rms_mm(mem, norm_memsrc[i], mem_w_kv[i].astype(BF16)).reshape(bp, mem_len, 2 * mem_dim)
        kv_p.append(kv)
        wq = mem_w_q[i].astype(BF16)
        wo = mem_w_o[i].astype(BF16)
        x = mem_attn(x, norm_mem[i], wq, kv, kv, wo, row0=0, nb=bp, t=tp, tq=min(MEM_TQ, tp),
                     kmap=lambda b, q: (b, 0, 0), vmap=lambda b, q: (b, 0, 1), kblk=mem_dim)
        x = mem_attn(x, norm_mem[i], wq, cache_mem_k[i].reshape(bs, mem_len, mem_dim),
                     cache_mem_v[i].reshape(bs, mem_len, mem_dim), wo, row0=mp, nb=bs, t=ts, tq=ts,
                     kmap=lambda b, q: (b, 0, 0), vmap=lambda b, q: (b, 0, 0), kblk=mem_dim)

        u = rms_mm(x, norm_ffn[i], w_up, lyr=i, act="relu2", out_dtype=BF16)
        x = mm_res(u, w_down, x, lyr=i, tm=MM_TM, tn=MM_TN, tk=MM_TK)

    y_p, y_s = rmsnorm_split(x, norm_final, mp)
    kv_all = jnp.stack(kv_p)
    mem_shape = (depth, bp, mem_len, MEM_HEADS, MEM_HEAD_DIM)
    stack = lambda items, k: jnp.stack([it[k] for it in items])
    return (
        y_p.reshape(bp, tp, d), y_s.reshape(bs, ts, d),
        stack(gdn_p, 0), stack(gdn_p, 1), stack(fox_p, 0), stack(fox_p, 1), stack(fox_p, 2), jnp.stack(gla_p),
        kv_all[..., :mem_dim].reshape(mem_shape), kv_all[..., mem_dim:].reshape(mem_shape),
        stack(gdn_s, 0), stack(gdn_s, 1), stack(fox_s, 0), stack(fox_s, 1), stack(fox_s, 2), jnp.stack(gla_s),
    )
```
